```python
import jax, jax.numpy as jnp
from jax import lax
import numpy as np

D_MODEL = 1024
BATCH = 8
SEQ = 4096
DEPTH = 2

D_FF = 2816
D_MIX = D_MODEL
HEAD_DIM = 64
N_HEADS = 8
N_KV_HEADS = 2
GROUP = N_HEADS // N_KV_HEADS
ATTN_W = N_HEADS * HEAD_DIM
KV_W = N_KV_HEADS * HEAD_DIM
CONV_CH = D_MIX - ATTN_W
CONV_WIDTH = 31
WINDOW = 128
BLOCK = 128
ROT_DIM = HEAD_DIM // 4
ROPE_THETA = 500000.0
PLE_DIM = 256
EPS = 1e-6
IN_W = ATTN_W + 2 * KV_W + 2 * CONV_CH
SPLITS = (ATTN_W, ATTN_W + KV_W, ATTN_W + 2 * KV_W, ATTN_W + 2 * KV_W + CONV_CH)

kernel_name = "hymba_swa_sink_conformer_conv_macaron"


def _rms_norm(x, g):
    xf = x.astype(jnp.float32)
    y = xf * lax.rsqrt(jnp.mean(xf * xf, axis=-1, keepdims=True) + EPS)
    return (y * g.astype(jnp.float32)).astype(x.dtype)


def _layer_norm(x, g, b):
    xf = x.astype(jnp.float32)
    mu = jnp.mean(xf, axis=-1, keepdims=True)
    xc = xf - mu
    y = xc * lax.rsqrt(jnp.mean(xc * xc, axis=-1, keepdims=True) + EPS)
    return (y * g.astype(jnp.float32) + b.astype(jnp.float32)).astype(x.dtype)


def _swiglu_ffn(x, w_gate_up, w_down):
    g, u = jnp.split(x @ w_gate_up, 2, axis=-1)
    return (jax.nn.silu(g) * u) @ w_down


def _partial_rope(x, positions):
    half = ROT_DIM // 2
    inv_freq = ROPE_THETA ** (-jnp.arange(0, ROT_DIM, 2, dtype=jnp.float32) / ROT_DIM)
    ang = positions.astype(jnp.float32)[..., None] * inv_freq
    cos = jnp.cos(ang)[:, :, None, :]
    sin = jnp.sin(ang)[:, :, None, :]
    xr = x[..., :ROT_DIM].astype(jnp.float32)
    x1, x2 = xr[..., :half], xr[..., half:]
    rot = jnp.concatenate([x1 * cos - x2 * sin, x2 * cos + x1 * sin], axis=-1).astype(x.dtype)
    return jnp.concatenate([rot, x[..., ROT_DIM:]], axis=-1)


def _sliding_window_attention(q, k, v, sinks):
    B, S, _, hd = q.shape
    nb = S // BLOCK
    qb = q.reshape(B, nb, BLOCK, N_KV_HEADS, GROUP, hd)
    kb = k.reshape(B, nb, BLOCK, N_KV_HEADS, hd)
    vb = v.reshape(B, nb, BLOCK, N_KV_HEADS, hd)
    pad = ((0, 0), (1, 0), (0, 0), (0, 0), (0, 0))
    kk = jnp.concatenate([jnp.pad(kb, pad)[:, :-1], kb], axis=2)
    vv = jnp.concatenate([jnp.pad(vb, pad)[:, :-1], vb], axis=2)
    s = jnp.einsum('bnqkgd,bnskd->bnkgqs', qb, kk).astype(jnp.float32) * (hd ** -0.5)
    qi = jnp.arange(BLOCK)[:, None] + BLOCK
    kj = jnp.arange(2 * BLOCK)[None, :]
    rel = qi - kj
    band = (rel >= 0) & (rel < WINDOW)
    valid = (jnp.arange(nb)[:, None, None] > 0) | (kj[None] >= BLOCK)
    mask = band[None] & valid
    s = jnp.where(mask[None, :, None, None], s, -1e30)
    sink = sinks.astype(jnp.float32).reshape(N_KV_HEADS, GROUP)[None, None, :, :, None, None]
    m = jnp.maximum(jnp.max(s, axis=-1, keepdims=True), sink)
    e = jnp.exp(s - m)
    probs = (e / (jnp.sum(e, axis=-1, keepdims=True) + jnp.exp(sink - m))).astype(v.dtype)
    o = jnp.einsum('bnkgqs,bnskd->bnqkgd', probs, vv)
    return o.reshape(B, S, N_HEADS * hd)


def _conformer_conv(a, g, w_dw, b_dw, ln_g, ln_b):
    u = a * jax.nn.sigmoid(g)
    y = lax.conv_general_dilated(
        u, w_dw[:, None, :].astype(u.dtype), window_strides=(1,),
        padding=[(CONV_WIDTH - 1, 0)],
        dimension_numbers=('NWC', 'WIO', 'NWC'),
        feature_group_count=CONV_CH) + b_dw
    return jax.nn.silu(_layer_norm(y, ln_g, ln_b))


def setup_inputs(seed: int = 0) -> dict:
    key = jax.random.key(seed)
    ks = jax.random.split(key, 24)
    f32 = jnp.float32

    def w(k, shape, fan_in, scale=1.0):
        return jax.random.normal(k, shape, f32) * (scale * fan_in ** -0.5)

    def gain(k, shape):
        return 1.0 + 0.02 * jax.random.normal(k, shape, f32)

    L = DEPTH
    x = jax.random.normal(ks[0], (BATCH, SEQ, D_MODEL), f32)
    p = jax.random.normal(ks[1], (DEPTH, BATCH, SEQ, PLE_DIM), f32)
    offsets = jax.random.randint(ks[2], (BATCH, 1), 0, 1024, dtype=jnp.int32)
    positions = offsets + jnp.arange(SEQ, dtype=jnp.int32)[None, :]
    return {
        "x": x,
        "p": p,
        "positions": positions,
        "ffn1_norm": gain(ks[3], (L, D_MODEL)),
        "ffn1_w_gate_up": w(ks[4], (L, D_MODEL, 2 * D_FF), D_MODEL),
        "ffn1_w_down": w(ks[5], (L, D_FF, D_MODEL), D_FF),
        "mix_norm": gain(ks[6], (L, D_MODEL)),
        "w_in": w(ks[7], (L, D_MODEL, IN_W), D_MODEL),
        "sinks": 0.5 * jax.random.normal(ks[8], (L, N_HEADS), f32),
        "conv_w": w(ks[9], (L, CONV_WIDTH, CONV_CH), CONV_WIDTH),
        "conv_b": 0.02 * jax.random.normal(ks[10], (L, CONV_CH), f32),
        "conv_ln_g": gain(ks[11], (L, CONV_CH)),
        "conv_ln_b": 0.02 * jax.random.normal(ks[12], (L, CONV_CH), f32),
        "attn_out_norm": gain(ks[13], (L, ATTN_W)),
        "conv_out_norm": gain(ks[14], (L, CONV_CH)),
        "w_out": w(ks[15], (L, D_MIX, D_MODEL), D_MIX),
        "ffn2_norm": gain(ks[16], (L, D_MODEL)),
        "ffn2_w_gate_up": w(ks[17], (L, D_MODEL, 2 * D_FF), D_MODEL),
        "ffn2_w_down": w(ks[18], (L, D_FF, D_MODEL), D_FF),
        "w_ple": w(ks[19], (L, PLE_DIM, D_MODEL), PLE_DIM),
        "ple_norm": gain(ks[20], (L, D_MODEL)),
        "w_ple_gate": w(ks[21], (L, D_MODEL, D_MODEL), D_MODEL),
        "final_norm": gain(ks[22], (D_MODEL,)),
    }


def reference(x, p, positions, ffn1_norm, ffn1_w_gate_up, ffn1_w_down, mix_norm, w_in,
              sinks, conv_w, conv_b, conv_ln_g, conv_ln_b, attn_out_norm, conv_out_norm,
              w_out, ffn2_norm, ffn2_w_gate_up, ffn2_w_down, w_ple, ple_norm, w_ple_gate,
              final_norm):
    B, S, _ = x.shape
    h = x
    for i in range(DEPTH):
        h = h + 0.5 * _swiglu_ffn(_rms_norm(h, ffn1_norm[i]), ffn1_w_gate_up[i], ffn1_w_down[i])
        u = _rms_norm(h, mix_norm[i])
        q, k, v, ca, cg = jnp.split(u @ w_in[i], SPLITS, axis=-1)
        q = _partial_rope(q.reshape(B, S, N_HEADS, HEAD_DIM), positions)
        k = _partial_rope(k.reshape(B, S, N_KV_HEADS, HEAD_DIM), positions)
        v = v.reshape(B, S, N_KV_HEADS, HEAD_DIM)
        attn = _sliding_window_attention(q, k, v, sinks[i])
        conv = _conformer_conv(ca, cg, conv_w[i], conv_b[i], conv_ln_g[i], conv_ln_b[i])
        mixed = jnp.concatenate([_rms_norm(attn, attn_out_norm[i]),
                                 _rms_norm(conv, conv_out_norm[i])], axis=-1)
        h = h + mixed @ w_out[i]
        h = h + 0.5 * _swiglu_ffn(_rms_norm(h, ffn2_norm[i]), ffn2_w_gate_up[i], ffn2_w_down[i])
        e = _rms_norm(p[i] @ w_ple[i], ple_norm[i])
        h = h + jax.nn.sigmoid(h @ w_ple_gate[i]) * e
    return _rms_norm(h, final_norm)
```

```python
import functools

import jax
import jax.numpy as jnp
from jax import lax
from jax.experimental import pallas as pl
from jax.experimental.pallas import tpu as pltpu

F32 = jnp.float32
BF16 = jnp.bfloat16

HEAD_DIM = 64
N_HEADS = 8
N_KV_HEADS = 2
GROUP = N_HEADS // N_KV_HEADS
ATTN_W = N_HEADS * HEAD_DIM
KV_W = N_KV_HEADS * HEAD_DIM
CONV_WIDTH = 31
WINDOW = 128
ROT_DIM = HEAD_DIM // 4
ROPE_THETA = 500000.0
EPS = 1e-6
MASK_VALUE = -1e30

LANES = 128
SUBLANES = 8

FFN_ROWS = 1024
FFN_CHUNK = 256
PROJ_ROWS = 1024
MIX_ROWS = 512
CONV_SUB = 64
CONV_HALO = 32
TABLE_ROWS = 2048
VMEM_LIMIT = 56 * 1024 * 1024


def _rms(x, g):
    return x * lax.rsqrt(jnp.mean(x * x, axis=-1, keepdims=True) + EPS) * g


def _dot(a, b):
    return jnp.dot(a, b, preferred_element_type=F32)


def _dot_nt(a, b):
    return lax.dot_general(a, b, (((1,), (1,)), ((), ())), preferred_element_type=F32)


def _resident(shape):
    zeros = (0,) * len(shape)
    return pl.BlockSpec(shape, lambda *_: zeros, pipeline_mode=pl.Buffered(1))


def _params(n_axes):
    return pltpu.CompilerParams(
        dimension_semantics=("arbitrary",) * n_axes, vmem_limit_bytes=VMEM_LIMIT)


def _rope_table_body(pos_ref, invf_ref, cos_ref, sin_ref):
    ang = pos_ref[...] * invf_ref[...]
    cos_ref[...] = jnp.cos(ang)
    sin_ref[...] = jnp.sin(ang)


def _rope_tables(pos, invf_lane):
    n = pos.shape[0]
    out = jax.ShapeDtypeStruct((n, LANES), F32)
    return pl.pallas_call(
        _rope_table_body,
        out_shape=(out, out),
        grid=(n // TABLE_ROWS,),
        in_specs=[pl.BlockSpec((TABLE_ROWS, 1), lambda i: (i, 0)),
                  pl.BlockSpec((1, LANES), lambda i: (0, 0))],
        out_specs=(pl.BlockSpec((TABLE_ROWS, LANES), lambda i: (i, 0)),
                   pl.BlockSpec((TABLE_ROWS, LANES), lambda i: (i, 0))),
        compiler_params=_params(1),
        name="rope_table",
    )(pos, invf_lane)


def _ffn_accumulate(x_ref, g_ref, wgu_ref, wd_ref, xn_ref, acc_ref):
    n_chunks, _, two_f = wgu_ref.shape
    f = two_f // 2
    xn_ref[...] = _rms(x_ref[...], g_ref[...]).astype(BF16)
    acc_ref[...] = jnp.zeros_like(acc_ref)

    def chunk(c, carry):
        gu = _dot(xn_ref[...], wgu_ref[c])
        g = gu[:, :f]
        a = (g * jax.nn.sigmoid(g) * gu[:, f:]).astype(BF16)
        acc_ref[...] += _dot(a, wd_ref[c])
        return carry

    lax.fori_loop(0, n_chunks, chunk, 0)


def _ffn_body(x_ref, g_ref, wgu_ref, wd_ref, o_ref, xn_ref, acc_ref):
    _ffn_accumulate(x_ref, g_ref, wgu_ref, wd_ref, xn_ref, acc_ref)
    o_ref[...] = x_ref[...] + 0.5 * acc_ref[...]


def _ffn(h, g, wgu, wd):
    n, d = h.shape
    row = pl.BlockSpec((FFN_ROWS, d), lambda i: (i, 0))
    return pl.pallas_call(
        _ffn_body,
        out_shape=jax.ShapeDtypeStruct((n, d), F32),
        grid=(n // FFN_ROWS,),
        in_specs=[row, _resident(g.shape), _resident(wgu.shape), _resident(wd.shape)],
        out_specs=row,
        scratch_shapes=[pltpu.VMEM((FFN_ROWS, d), BF16), pltpu.VMEM((FFN_ROWS, d), F32)],
        compiler_params=_params(1),
        name="ffn",
    )(h, g, wgu, wd)


def _ffn_ple_body(x_ref, g_ref, wgu_ref, wd_ref, p_ref, wple_ref, gple_ref, wgate_ref, gfin_ref,
                  o_ref, xn_ref, acc_ref, *, final):
    _ffn_accumulate(x_ref, g_ref, wgu_ref, wd_ref, xn_ref, acc_ref)
    h = x_ref[...] + 0.5 * acc_ref[...]
    e = _rms(_dot(p_ref[...].astype(BF16), wple_ref[...]), gple_ref[...])
    h = h + jax.nn.sigmoid(_dot(h.astype(BF16), wgate_ref[...])) * e
    if final:
        h = _rms(h, gfin_ref[...])
    o_ref[...] = h


def _ffn_ple(h, g, wgu, wd, p, wple, gple, wgate, gfin, *, final):
    n, d = h.shape
    row = pl.BlockSpec((FFN_ROWS, d), lambda i: (i, 0))
    return pl.pallas_call(
        functools.partial(_ffn_ple_body, final=final),
        out_shape=jax.ShapeDtypeStruct((n, d), F32),
        grid=(n // FFN_ROWS,),
        in_specs=[row, _resident(g.shape), _resident(wgu.shape), _resident(wd.shape),
                  pl.BlockSpec((FFN_ROWS, p.shape[1]), lambda i: (i, 0)),
                  _resident(wple.shape), _resident(gple.shape), _resident(wgate.shape),
                  _resident(gfin.shape)],
        out_specs=row,
        scratch_shapes=[pltpu.VMEM((FFN_ROWS, d), BF16), pltpu.VMEM((FFN_ROWS, d), F32)],
        compiler_params=_params(1),
        name="ffn_ple",
    )(h, g, wgu, wd, p, wple, gple, wgate, gfin)


def _in_proj_body(h_ref, g_ref, wq_ref, wkv_ref, wag_ref, cos_ref, sin_ref,
                  q_ref, kx_ref, vx_ref, u_ref):
    xn = _rms(h_ref[...], g_ref[...]).astype(BF16)
    cos = cos_ref[...]
    sin = sin_ref[...]
    lane = lax.broadcasted_iota(jnp.int32, (1, LANES), 1)
    in_head = lane % HEAD_DIM
    half = ROT_DIM // 2
    s_lo = jnp.where(in_head < half, -sin, 0.0)
    s_hi = jnp.where((in_head >= half) & (in_head < ROT_DIM), sin, 0.0)

    def rope(t):
        return (t * cos + pltpu.roll(t, LANES - half, 1) * s_lo + pltpu.roll(t, half, 1) * s_hi)

    scale = HEAD_DIM ** -0.5
    q = _dot(xn, wq_ref[...])
    for p in range(ATTN_W // LANES):
        cols = slice(p * LANES, (p + 1) * LANES)
        q_ref[:, cols] = (rope(q[:, cols]) * scale).astype(BF16)

    kv = _dot(xn, wkv_ref[...])
    low = lane < HEAD_DIM

    def head_pair_layout(t, out_ref):
        sw = pltpu.roll(t, HEAD_DIM, 1)
        out_ref[:, 0 * LANES:1 * LANES] = jnp.where(low, t, 0.0).astype(BF16)
        out_ref[:, 1 * LANES:2 * LANES] = jnp.where(low, 0.0, sw).astype(BF16)
        out_ref[:, 2 * LANES:3 * LANES] = jnp.where(low, sw, 0.0).astype(BF16)
        out_ref[:, 3 * LANES:4 * LANES] = jnp.where(low, 0.0, t).astype(BF16)

    head_pair_layout(rope(kv[:, :KV_W]), kx_ref)
    head_pair_layout(kv[:, KV_W:], vx_ref)

    ag = _dot(xn, wag_ref[...])
    c = ag.shape[1] // 2
    u_ref[...] = ag[:, :c] * jax.nn.sigmoid(ag[:, c:])


def _in_proj(h, g, wq, wkv, wag, cos, sin):
    n, d = h.shape
    rows = lambda w: pl.BlockSpec((PROJ_ROWS, w), lambda i: (i, 0))
    c = wag.shape[1] // 2
    return pl.pallas_call(
        _in_proj_body,
        out_shape=(jax.ShapeDtypeStruct((n, ATTN_W), BF16),
                   jax.ShapeDtypeStruct((n, 4 * LANES), BF16),
                   jax.ShapeDtypeStruct((n, 4 * LANES), BF16),
                   jax.ShapeDtypeStruct((n, c), F32)),
        grid=(n // PROJ_ROWS,),
        in_specs=[rows(d), _resident(g.shape), _resident(wq.shape), _resident(wkv.shape),
                  _resident(wag.shape), rows(LANES), rows(LANES)],
        out_specs=(rows(ATTN_W), rows(4 * LANES), rows(4 * LANES), rows(c)),
        compiler_params=_params(1),
        name="in_proj",
    )(h, g, wq, wkv, wag, cos, sin)


def _mix_body(sinks_ref, h_ref, q_ref, kc_ref, kp_ref, vc_ref, vp_ref, uc_ref, up_ref,
              cw_ref, cb_ref, lng_ref, lnb_ref, an_ref, cn_ref, wout_ref,
              o_ref, kall_ref, vall_ref, uall_ref, attn_ref, mixed_ref):
    i = pl.program_id(1)
    ts = h_ref.shape[0]
    conv_ch = uc_ref.shape[1]
    not_first = i > 0

    kall_ref[:WINDOW] = kp_ref[...]
    kall_ref[WINDOW:] = kc_ref[...]
    vall_ref[:WINDOW] = vp_ref[...]
    vall_ref[WINDOW:] = vc_ref[...]
    uall_ref[:CONV_HALO] = jnp.where(not_first, up_ref[...], 0.0)
    uall_ref[CONV_HALO:] = uc_ref[...]

    r = lax.broadcasted_iota(jnp.int32, (WINDOW, 2 * WINDOW), 0)
    c = lax.broadcasted_iota(jnp.int32, (WINDOW, 2 * WINDOW), 1)
    band = (c > r) & (c <= r + WINDOW)
    band_first = band & ((c >= WINDOW) | not_first)

    def softmax_sink(s, mask, sink):
        s = jnp.where(mask, s, MASK_VALUE)
        m = jnp.maximum(jnp.max(s, axis=-1, keepdims=True), sink)
        e = jnp.exp(s - m)
        den = jnp.sum(e, axis=-1, keepdims=True) + jnp.exp(sink - m)
        return (e * (1.0 / den)).astype(BF16)

    for qb in range(ts // WINDOW):
        rows = slice(qb * WINDOW, (qb + 1) * WINDOW)
        keys = slice(qb * WINDOW, (qb + 2) * WINDOW)
        mask = band_first if qb == 0 else band
        for p in range(N_HEADS // 2):
            grp = (2 * p) // GROUP
            lo = slice((2 * grp) * LANES, (2 * grp + 1) * LANES)
            hi = slice((2 * grp + 1) * LANES, (2 * grp + 2) * LANES)
            qp = q_ref[rows, p * LANES:(p + 1) * LANES]
            p_even = softmax_sink(_dot_nt(qp, kall_ref[keys, lo]), mask, sinks_ref[2 * p])
            p_odd = softmax_sink(_dot_nt(qp, kall_ref[keys, hi]), mask, sinks_ref[2 * p + 1])
            attn_ref[rows, p * LANES:(p + 1) * LANES] = (
                _dot(p_even, vall_ref[keys, lo]) + _dot(p_odd, vall_ref[keys, hi]))

    mixed_ref[:, :ATTN_W] = _rms(attn_ref[...], an_ref[...]).astype(BF16)

    first_tap = CONV_HALO - (CONV_WIDTH - 1)
    for sb in range(ts // CONV_SUB):
        base = sb * CONV_SUB
        y = jnp.broadcast_to(cb_ref[...], (CONV_SUB, conv_ch))
        for k in range(CONV_WIDTH):
            y = y + cw_ref[k:k + 1, :] * uall_ref[base + first_tap + k:base + first_tap + k + CONV_SUB, :]
        mu = jnp.mean(y, axis=-1, keepdims=True)
        yc = y - mu
        yn = yc * lax.rsqrt(jnp.mean(yc * yc, axis=-1, keepdims=True) + EPS) * lng_ref[...] + lnb_ref[...]
        z = yn * jax.nn.sigmoid(yn)
        mixed_ref[base:base + CONV_SUB, ATTN_W:] = _rms(z, cn_ref[...]).astype(BF16)

    o_ref[...] = h_ref[...] + _dot(mixed_ref[...], wout_ref[...])


def _mix(h, q, kx, vx, u, sinks, cw, cb, lng, lnb, an, cn, wout):
    b, s, d = h.shape
    conv_ch = u.shape[2]
    ts = MIX_ROWS
    cur = lambda w: pl.BlockSpec((None, ts, w), lambda bi, i: (bi, i, 0))
    prev = lambda rows, w: pl.BlockSpec(
        (None, rows, w), lambda bi, i: (bi, jnp.maximum(i * (ts // rows) - 1, 0), 0))
    return pl.pallas_call(
        _mix_body,
        out_shape=jax.ShapeDtypeStruct((b, s, d), F32),
        grid=(b, s // ts),
        in_specs=[pl.BlockSpec(memory_space=pltpu.SMEM),
                  cur(d), cur(ATTN_W),
                  cur(4 * LANES), prev(WINDOW, 4 * LANES),
                  cur(4 * LANES), prev(WINDOW, 4 * LANES),
                  cur(conv_ch), prev(CONV_HALO, conv_ch),
                  _resident(cw.shape), _resident(cb.shape), _resident(lng.shape),
                  _resident(lnb.shape), _resident(an.shape), _resident(cn.shape),
                  _resident(wout.shape)],
        out_specs=cur(d),
        scratch_shapes=[pltpu.VMEM((ts + WINDOW, 4 * LANES), BF16),
                        pltpu.VMEM((ts + WINDOW, 4 * LANES), BF16),
                        pltpu.VMEM((ts + CONV_HALO, conv_ch), F32),
                        pltpu.VMEM((ts, ATTN_W), F32),
                        pltpu.VMEM((ts, d), BF16)],
        compiler_params=_params(2),
        name="mix",
    )(sinks, h, q, kx, kx, vx, vx, u, u, cw, cb, lng, lnb, an, cn, wout)


def _chunk_ffn_weights(w_gate_up, w_down):
    d, two_ff = w_gate_up.shape
    ff = two_ff // 2
    nc = ff // FFN_CHUNK
    gate = w_gate_up[:, :ff].reshape(d, nc, FFN_CHUNK)
    up = w_gate_up[:, ff:].reshape(d, nc, FFN_CHUNK)
    wgu = jnp.concatenate([gate, up], axis=2).transpose(1, 0, 2).astype(BF16)
    wd = w_down.reshape(nc, FFN_CHUNK, d).astype(BF16)
    return wgu, wd


def kernel(x, p, positions, ffn1_norm, ffn1_w_gate_up, ffn1_w_down, mix_norm, w_in, sinks, conv_w, conv_b, conv_ln_g, conv_ln_b, attn_out_norm, conv_out_norm, w_out, ffn2_norm, ffn2_w_gate_up, ffn2_w_down, w_ple, ple_norm, w_ple_gate, final_norm):
    b, s, d = x.shape
    depth = p.shape[0]
    n = b * s
    row = lambda v: v.reshape(1, -1)

    inv_freq = ROPE_THETA ** (-jnp.arange(0, ROT_DIM, 2, dtype=F32) / ROT_DIM)
    lane = jnp.arange(LANES) % HEAD_DIM
    invf_lane = jnp.where(lane < ROT_DIM, inv_freq[lane % (ROT_DIM // 2)], 0.0).reshape(1, LANES)
    cos, sin = _rope_tables(positions.astype(F32).reshape(n, 1), invf_lane)

    h = x.reshape(n, d)
    for i in range(depth):
        wgu1, wd1 = _chunk_ffn_weights(ffn1_w_gate_up[i], ffn1_w_down[i])
        wgu2, wd2 = _chunk_ffn_weights(ffn2_w_gate_up[i], ffn2_w_down[i])
        win = w_in[i].astype(BF16)
        wq = win[:, :ATTN_W]
        wkv = win[:, ATTN_W:ATTN_W + 2 * KV_W]
        wag = win[:, ATTN_W + 2 * KV_W:]

        h = _ffn(h, row(ffn1_norm[i]), wgu1, wd1)
        q, kx, vx, u = _in_proj(h, row(mix_norm[i]), wq, wkv, wag, cos, sin)
        shp = lambda t: t.reshape(b, s, t.shape[1])
        h = _mix(shp(h), shp(q), shp(kx), shp(vx), shp(u), sinks[i], conv_w[i], row(conv_b[i]),
                 row(conv_ln_g[i]), row(conv_ln_b[i]), row(attn_out_norm[i]),
                 row(conv_out_norm[i]), w_out[i].astype(BF16)).reshape(n, d)
        h = _ffn_ple(h, row(ffn2_norm[i]), wgu2, wd2, p[i].reshape(n, -1), w_ple[i].astype(BF16),
                     row(ple_norm[i]), w_ple_gate[i].astype(BF16), row(final_norm),
                     final=(i == depth - 1))
    return h.reshape(b, s, d)
```

```python
import functools

import jax
import jax.numpy as jnp
from jax import lax
from jax.experimental import pallas as pl
from jax.experimental.pallas import tpu as pltpu

F32 = jnp.float32
BF16 = jnp.bfloat16

HEAD_DIM = 64
N_HEADS = 8
N_KV_HEADS = 2
GROUP = N_HEADS // N_KV_HEADS
ATTN_W = N_HEADS * HEAD_DIM
KV_W = N_KV_HEADS * HEAD_DIM
CONV_WIDTH = 31
WINDOW = 128
ROT_DIM = HEAD_DIM // 4
ROPE_THETA = 500000.0
EPS = 1e-6
MASK_VALUE = -1e30

LANES = 128
SUBLANES = 8

FFN_ROWS = 1024
FFN_CHUNK = 256
PROJ_ROWS = 1024
MIX_ROWS = 512
CONV_SUB = 64
CONV_HALO = 32
TABLE_ROWS = 2048
VMEM_LIMIT = 56 * 1024 * 1024


def _rms(x, g):
    return x * lax.rsqrt(jnp.mean(x * x, axis=-1, keepdims=True) + EPS) * g


def _dot(a, b):
    return jnp.dot(a, b, preferred_element_type=F32)


def _dot_nt(a, b):
    return lax.dot_general(a, b, (((1,), (1,)), ((), ())), preferred_element_type=F32)


def _resident(shape):
    zeros = (0,) * len(shape)
    return pl.BlockSpec(shape, lambda *_: zeros, pipeline_mode=pl.Buffered(1))


def _params(n_axes):
    return pltpu.CompilerParams(
        dimension_semantics=("arbitrary",) * n_axes, vmem_limit_bytes=VMEM_LIMIT)


def _rope_table_body(pos_ref, invf_ref, cos_ref, sin_ref):
    ang = pos_ref[...] * invf_ref[...]
    cos_ref[...] = jnp.cos(ang)
    sin_ref[...] = jnp.sin(ang)


def _rope_tables(pos, invf_lane):
    n = pos.shape[0]
    out = jax.ShapeDtypeStruct((n, LANES), F32)
    return pl.pallas_call(
        _rope_table_body,
        out_shape=(out, out),
        grid=(n // TABLE_ROWS,),
        in_specs=[pl.BlockSpec((TABLE_ROWS, 1), lambda i: (i, 0)),
                  pl.BlockSpec((1, LANES), lambda i: (0, 0))],
        out_specs=(pl.BlockSpec((TABLE_ROWS, LANES), lambda i: (i, 0)),
                   pl.BlockSpec((TABLE_ROWS, LANES), lambda i: (i, 0))),
        compiler_params=_params(1),
        name="rope_table",
    )(pos, invf_lane)


def _ffn_accumulate(x_ref, g_ref, wgu_ref, wd_ref, xn_ref, acc_ref):
    n_chunks, _, two_f = wgu_ref.shape
    f = two_f // 2
    xn_ref[...] = _rms(x_ref[...], g_ref[...]).astype(BF16)
    acc_ref[...] = jnp.zeros_like(acc_ref)

    def chunk(c, carry):
        gu = _dot(xn_ref[...], wgu_ref[c])
        g = gu[:, :f]
        a = (g * jax.nn.sigmoid(g) * gu[:, f:]).astype(BF16)
        acc_ref[...] += _dot(a, wd_ref[c])
        return carry

    lax.fori_loop(0, n_chunks, chunk, 0)


def _ffn_body(x_ref, g_ref, wgu_ref, wd_ref, o_ref, xn_ref, acc_ref):
    _ffn_accumulate(x_ref, g_ref, wgu_ref, wd_ref, xn_ref, acc_ref)
    o_ref[...] = x_ref[...] + 0.5 * acc_ref[...]


def _ffn(h, g, wgu, wd):
    n, d = h.shape
    row = pl.BlockSpec((FFN_ROWS, d), lambda i: (i, 0))
    return pl.pallas_call(
        _ffn_body,
        out_shape=jax.ShapeDtypeStruct((n, d), F32),
        grid=(n // FFN_ROWS,),
        in_specs=[row, _resident(g.shape), _resident(wgu.shape), _resident(wd.shape)],
        out_specs=row,
        scratch_shapes=[pltpu.VMEM((FFN_ROWS, d), BF16), pltpu.VMEM((FFN_ROWS, d), F32)],
        compiler_params=_params(1),
        name="ffn",
    )(h, g, wgu, wd)


def _ffn_ple_body(x_ref, g_ref, wgu_ref, wd_ref, p_ref, wple_ref, gple_ref, wgate_ref, gfin_ref,
                  o_ref, xn_ref, acc_ref, *, final):
    _ffn_accumulate(x_ref, g_ref, wgu_ref, wd_ref, xn_ref, acc_ref)
    h = x_ref[...] + 0.5 * acc_ref[...]
    e = _rms(_dot(p_ref[...].astype(BF16), wple_ref[...]), gple_ref[...])
    h = h + jax.nn.sigmoid(_dot(h.astype(BF16), wgate_ref[...])) * e
    if final:
        h = _rms(h, gfin_ref[...])
    o_ref[...] = h


def _ffn_ple(h, g, wgu, wd, p, wple, gple, wgate, gfin, *, final):
    n, d = h.shape
    row = pl.BlockSpec((FFN_ROWS, d), lambda i: (i, 0))
    return pl.pallas_call(
        functools.partial(_ffn_ple_body, final=final),
        out_shape=jax.ShapeDtypeStruct((n, d), F32),
        grid=(n // FFN_ROWS,),
        in_specs=[row, _resident(g.shape), _resident(wgu.shape), _resident(wd.shape),
                  pl.BlockSpec((FFN_ROWS, p.shape[1]), lambda i: (i, 0)),
                  _resident(wple.shape), _resident(gple.shape), _resident(wgate.shape),
                  _resident(gfin.shape)],
        out_specs=row,
        scratch_shapes=[pltpu.VMEM((FFN_ROWS, d), BF16), pltpu.VMEM((FFN_ROWS, d), F32)],
        compiler_params=_params(1),
        name="ffn_ple",
    )(h, g, wgu, wd, p, wple, gple, wgate, gfin)


def _in_proj_body(h_ref, g_ref, wq_ref, wkv_ref, wag_ref, cos_ref, sin_ref,
                  q_ref, kx_ref, vx_ref, u_ref):
    xn = _rms(h_ref[...], g_ref[...]).astype(BF16)
    cos = cos_ref[...]
    sin = sin_ref[...]
    lane = lax.broadcasted_iota(jnp.int32, (1, LANES), 1)
    in_head = lane % HEAD_DIM
    half = ROT_DIM // 2
    s_lo = jnp.where(in_head < half, -sin, 0.0)
    s_hi = jnp.where((in_head >= half) & (in_head < ROT_DIM), sin, 0.0)

    def rope(t):
        return (t * cos + pltpu.roll(t, LANES - half, 1) * s_lo + pltpu.roll(t, half, 1) * s_hi)

    scale = HEAD_DIM ** -0.5
    q = _dot(xn, wq_ref[...])
    for p in range(ATTN_W // LANES):
        cols = slice(p * LANES, (p + 1) * LANES)
        q_ref[:, cols] = (rope(q[:, cols]) * scale).astype(BF16)

    kv = _dot(xn, wkv_ref[...])
    low = lane < HEAD_DIM

    def head_pair_layout(t, out_ref):
        sw = pltpu.roll(t, HEAD_DIM, 1)
        out_ref[:, 0 * LANES:1 * LANES] = jnp.where(low, t, 0.0).astype(BF16)
        out_ref[:, 1 * LANES:2 * LANES] = jnp.where(low, 0.0, sw).astype(BF16)
        out_ref[:, 2 * LANES:3 * LANES] = jnp.where(low, sw, 0.0).astype(BF16)
        out_ref[:, 3 * LANES:4 * LANES] = jnp.where(low, 0.0, t).astype(BF16)

    head_pair_layout(rope(kv[:, :KV_W]), kx_ref)
    head_pair_layout(kv[:, KV_W:], vx_ref)

    ag = _dot(xn, wag_ref[...])
    c = ag.shape[1] // 2
    u_ref[...] = ag[:, :c] * jax.nn.sigmoid(ag[:, c:])


def _in_proj(h, g, wq, wkv, wag, cos, sin):
    n, d = h.shape
    rows = lambda w: pl.BlockSpec((PROJ_ROWS, w), lambda i: (i, 0))
    c = wag.shape[1] // 2
    return pl.pallas_call(
        _in_proj_body,
        out_shape=(jax.ShapeDtypeStruct((n, ATTN_W), BF16),
                   jax.ShapeDtypeStruct((n, 4 * LANES), BF16),
                   jax.ShapeDtypeStruct((n, 4 * LANES), BF16),
                   jax.ShapeDtypeStruct((n, c), F32)),
        grid=(n // PROJ_ROWS,),
        in_specs=[rows(d), _resident(g.shape), _resident(wq.shape), _resident(wkv.shape),
                  _resident(wag.shape), rows(LANES), rows(LANES)],
        out_specs=(rows(ATTN_W), rows(4 * LANES), rows(4 * LANES), rows(c)),
        compiler_params=_params(1),
        name="in_proj",
    )(h, g, wq, wkv, wag, cos, sin)


def _conv_rows(uall_ref, cw_ref, base):
    first_tap = CONV_HALO - (CONV_WIDTH - 1)
    y = None
    for r in range(SUBLANES):
        rows = CONV_SUB + (SUBLANES if r else 0)
        part = None
        for o in range(r, first_tap + CONV_WIDTH, SUBLANES):
            if o < first_tap:
                continue
            k = o - first_tap
            start = base + o - r
            term = cw_ref[k:k + 1, :] * uall_ref[start:start + rows, :]
            part = term if part is None else part + term
        part = part[r:r + CONV_SUB, :]
        y = part if y is None else y + part
    return y


def _mix_body(sinks_ref, h_ref, q_ref, kc_ref, kp_ref, vc_ref, vp_ref, uc_ref, up_ref,
              cw_ref, cb_ref, lng_ref, lnb_ref, an_ref, cn_ref, wout_ref,
              o_ref, kall_ref, vall_ref, uall_ref, mixed_ref):
    i = pl.program_id(1)
    ts = h_ref.shape[0]
    not_first = i > 0

    kall_ref[:WINDOW] = kp_ref[...]
    kall_ref[WINDOW:] = kc_ref[...]
    vall_ref[:WINDOW] = vp_ref[...]
    vall_ref[WINDOW:] = vc_ref[...]
    uall_ref[:CONV_HALO] = jnp.where(not_first, up_ref[...], 0.0)
    uall_ref[CONV_HALO:] = uc_ref[...]

    r = lax.broadcasted_iota(jnp.int32, (WINDOW, 2 * WINDOW), 0)
    c = lax.broadcasted_iota(jnp.int32, (WINDOW, 2 * WINDOW), 1)
    band = (c > r) & (c <= r + WINDOW)
    band_first = band & ((c >= WINDOW) | not_first)

    def kv_tiles(p):
        grp = (2 * p) // GROUP
        return (slice((2 * grp) * LANES, (2 * grp + 1) * LANES),
                slice((2 * grp + 1) * LANES, (2 * grp + 2) * LANES))

    def softmax_sink(s, mask, sink):
        s = jnp.where(mask, s, MASK_VALUE)
        m = jnp.maximum(jnp.max(s, axis=-1, keepdims=True), sink)
        e = jnp.exp(s - m)
        den = jnp.sum(e, axis=-1, keepdims=True) + jnp.exp(sink - m)
        return (e * (1.0 / den)).astype(BF16)

    def conv_module(sb):
        base = sb * CONV_SUB
        y = _conv_rows(uall_ref, cw_ref, base) + cb_ref[...]
        mu = jnp.mean(y, axis=-1, keepdims=True)
        yc = y - mu
        yn = yc * lax.rsqrt(jnp.mean(yc * yc, axis=-1, keepdims=True) + EPS) * lng_ref[...] + lnb_ref[...]
        z = yn * jax.nn.sigmoid(yn)
        mixed_ref[base:base + CONV_SUB, ATTN_W:] = _rms(z, cn_ref[...]).astype(BF16)

    conv_per_block = WINDOW // CONV_SUB
    for qb in range(ts // WINDOW):
        rows = slice(qb * WINDOW, (qb + 1) * WINDOW)
        keys = slice(qb * WINDOW, (qb + 2) * WINDOW)
        mask = band_first if qb == 0 else band
        scores = []
        for p in range(N_HEADS // 2):
            lo, hi = kv_tiles(p)
            qp = q_ref[rows, p * LANES:(p + 1) * LANES]
            scores.append(_dot_nt(qp, kall_ref[keys, lo]))
            scores.append(_dot_nt(qp, kall_ref[keys, hi]))
        conv_module(qb * conv_per_block)
        probs = [softmax_sink(s, mask, sinks_ref[hd]) for hd, s in enumerate(scores)]
        for sb in range(1, conv_per_block):
            conv_module(qb * conv_per_block + sb)
        outs = []
        for p in range(N_HEADS // 2):
            lo, hi = kv_tiles(p)
            outs.append(_dot(probs[2 * p], vall_ref[keys, lo]) + _dot(probs[2 * p + 1], vall_ref[keys, hi]))
        attn = jnp.concatenate(outs, axis=1)
        mixed_ref[rows, :ATTN_W] = _rms(attn, an_ref[...]).astype(BF16)

    o_ref[...] = h_ref[...] + _dot(mixed_ref[...], wout_ref[...])


def _mix(h, q, kx, vx, u, sinks, cw, cb, lng, lnb, an, cn, wout):
    b, s, d = h.shape
    conv_ch = u.shape[2]
    ts = MIX_ROWS
    cur = lambda w: pl.BlockSpec((None, ts, w), lambda bi, i: (bi, i, 0))
    prev = lambda rows, w: pl.BlockSpec(
        (None, rows, w), lambda bi, i: (bi, jnp.maximum(i * (ts // rows) - 1, 0), 0))
    return pl.pallas_call(
        _mix_body,
        out_shape=jax.ShapeDtypeStruct((b, s, d), F32),
        grid=(b, s // ts),
        in_specs=[pl.BlockSpec(memory_space=pltpu.SMEM),
                  cur(d), cur(ATTN_W),
                  cur(4 * LANES), prev(WINDOW, 4 * LANES),
                  cur(4 * LANES), prev(WINDOW, 4 * LANES),
                  cur(conv_ch), prev(CONV_HALO, conv_ch),
                  _resident(cw.shape), _resident(cb.shape), _resident(lng.shape),
                  _resident(lnb.shape), _resident(an.shape), _resident(cn.shape),
                  _resident(wout.shape)],
        out_specs=cur(d),
        scratch_shapes=[pltpu.VMEM((ts + WINDOW, 4 * LANES), BF16),
                        pltpu.VMEM((ts + WINDOW, 4 * LANES), BF16),
                        pltpu.VMEM((ts + CONV_HALO, conv_ch), F32),
                        pltpu.VMEM((ts, d), BF16)],
        compiler_params=_params(2),
        name="mix",
    )(sinks, h, q, kx, kx, vx, vx, u, u, cw, cb, lng, lnb, an, cn, wout)


def _chunk_ffn_weights(w_gate_up, w_down):
    d, two_ff = w_gate_up.shape
    ff = two_ff // 2
    nc = ff // FFN_CHUNK
    gate = w_gate_up[:, :ff].reshape(d, nc, FFN_CHUNK)
    up = w_gate_up[:, ff:].reshape(d, nc, FFN_CHUNK)
    wgu = jnp.concatenate([gate, up], axis=2).transpose(1, 0, 2).astype(BF16)
    wd = w_down.reshape(nc, FFN_CHUNK, d).astype(BF16)
    return wgu, wd


def kernel(x, p, positions, ffn1_norm, ffn1_w_gate_up, ffn1_w_down, mix_norm, w_in, sinks, conv_w, conv_b, conv_ln_g, conv_ln_b, attn_out_norm, conv_out_norm, w_out, ffn2_norm, ffn2_w_gate_up, ffn2_w_down, w_ple, ple_norm, w_ple_gate, final_norm):
    b, s, d = x.shape
    depth = p.shape[0]
    n = b * s
    row = lambda v: v.reshape(1, -1)

    inv_freq = ROPE_THETA ** (-jnp.arange(0, ROT_DIM, 2, dtype=F32) / ROT_DIM)
    lane = jnp.arange(LANES) % HEAD_DIM
    invf_lane = jnp.where(lane < ROT_DIM, inv_freq[lane % (ROT_DIM // 2)], 0.0).reshape(1, LANES)
    cos, sin = _rope_tables(positions.astype(F32).reshape(n, 1), invf_lane)

    h = x.reshape(n, d)
    for i in range(depth):
        wgu1, wd1 = _chunk_ffn_weights(ffn1_w_gate_up[i], ffn1_w_down[i])
        wgu2, wd2 = _chunk_ffn_weights(ffn2_w_gate_up[i], ffn2_w_down[i])
        win = w_in[i].astype(BF16)
        wq = win[:, :ATTN_W]
        wkv = win[:, ATTN_W:ATTN_W + 2 * KV_W]
        wag = win[:, ATTN_W + 2 * KV_W:]

        h = _ffn(h, row(ffn1_norm[i]), wgu1, wd1)
        q, kx, vx, u = _in_proj(h, row(mix_norm[i]), wq, wkv, wag, cos, sin)
        shp = lambda t: t.reshape(b, s, t.shape[1])
        h = _mix(shp(h), shp(q), shp(kx), shp(vx), shp(u), sinks[i], conv_w[i], row(conv_b[i]),
                 row(conv_ln_g[i]), row(conv_ln_b[i]), row(attn_out_norm[i]),
                 row(conv_out_norm[i]), w_out[i].astype(BF16)).reshape(n, d)
        h = _ffn_ple(h, row(ffn2_norm[i]), wgu2, wd2, p[i].reshape(n, -1), w_ple[i].astype(BF16),
                     row(ple_norm[i]), w_ple_gate[i].astype(BF16), row(final_norm),
                     final=(i == depth - 1))
    return h.reshape(b, s, d)
```

```python
import functools

import jax
import jax.numpy as jnp
from jax import lax
from jax.experimental import pallas as pl
from jax.experimental.pallas import tpu as pltpu

F32 = jnp.float32
BF16 = jnp.bfloat16

HEAD_DIM = 64
N_HEADS = 8
N_KV_HEADS = 2
GROUP = N_HEADS // N_KV_HEADS
ATTN_W = N_HEADS * HEAD_DIM
KV_W = N_KV_HEADS * HEAD_DIM
CONV_WIDTH = 31
WINDOW = 128
ROT_DIM = HEAD_DIM // 4
ROPE_THETA = 500000.0
EPS = 1e-6
MASK_VALUE = -1e30

LANES = 128
SUBLANES = 8

FFN_ROWS = 1024
FFN_CHUNK = 256
PROJ_ROWS = 1024
MIX_ROWS = 512
CONV_SUB = 64
CONV_HALO = 32
TABLE_ROWS = 2048
VMEM_LIMIT = 56 * 1024 * 1024


def _rms(x, g):
    return x * lax.rsqrt(jnp.mean(x * x, axis=-1, keepdims=True) + EPS) * g


def _dot(a, b):
    return jnp.dot(a, b, preferred_element_type=F32)


def _dot_nt(a, b):
    return lax.dot_general(a, b, (((1,), (1,)), ((), ())), preferred_element_type=F32)


def _layer_spec(arr, layer):
    tail = arr.shape[1:]
    idx = (layer,) + (0,) * len(tail)
    return pl.BlockSpec((None,) + tail, lambda *_: idx, pipeline_mode=pl.Buffered(1))


def _params(n_axes):
    return pltpu.CompilerParams(
        dimension_semantics=("arbitrary",) * n_axes, vmem_limit_bytes=VMEM_LIMIT)


def _rope_table_body(pos_ref, invf_ref, cos_ref, sin_ref):
    ang = pos_ref[...] * invf_ref[...]
    cos_ref[...] = jnp.cos(ang)
    sin_ref[...] = jnp.sin(ang)


def _rope_tables(pos, invf_lane):
    n = pos.shape[0]
    out = jax.ShapeDtypeStruct((n, LANES), F32)
    return pl.pallas_call(
        _rope_table_body,
        out_shape=(out, out),
        grid=(n // TABLE_ROWS,),
        in_specs=[pl.BlockSpec((TABLE_ROWS, 1), lambda i: (i, 0)),
                  pl.BlockSpec((1, LANES), lambda i: (0, 0))],
        out_specs=(pl.BlockSpec((TABLE_ROWS, LANES), lambda i: (i, 0)),
                   pl.BlockSpec((TABLE_ROWS, LANES), lambda i: (i, 0))),
        compiler_params=_params(1),
        name="rope_table",
    )(pos, invf_lane)


def _swiglu(x_ref, g_ref, wgu_ref, wd_ref, xn_ref, acc_ref):
    ff = wd_ref.shape[0]
    n_chunks = ff // FFN_CHUNK
    xn_ref[...] = _rms(x_ref[...], g_ref[...]).astype(BF16)
    y = None
    for c in range(n_chunks):
        lo = c * FFN_CHUNK
        xn = xn_ref[...]
        g = _dot(xn, wgu_ref[:, lo:lo + FFN_CHUNK])
        u = _dot(xn, wgu_ref[:, ff + lo:ff + lo + FFN_CHUNK])
        a = (g * jax.nn.sigmoid(g) * u).astype(BF16)
        d = _dot(a, wd_ref[lo:lo + FFN_CHUNK, :])
        if c == 0:
            acc_ref[...] = d
        elif c < n_chunks - 1:
            acc_ref[...] += d
        else:
            y = acc_ref[...] + d
    return y


def _ffn_body(x_ref, g_ref, wgu_ref, wd_ref, o_ref, xn_ref, acc_ref):
    y = _swiglu(x_ref, g_ref, wgu_ref, wd_ref, xn_ref, acc_ref)
    o_ref[...] = x_ref[...] + 0.5 * y


def _ffn(h, layer, g, wgu, wd):
    n, d = h.shape
    row = pl.BlockSpec((FFN_ROWS, d), lambda i: (i, 0))
    return pl.pallas_call(
        _ffn_body,
        out_shape=jax.ShapeDtypeStruct((n, d), F32),
        grid=(n // FFN_ROWS,),
        in_specs=[row, _layer_spec(g, layer), _layer_spec(wgu, layer), _layer_spec(wd, layer)],
        out_specs=row,
        scratch_shapes=[pltpu.VMEM((FFN_ROWS, d), BF16), pltpu.VMEM((FFN_ROWS, d), F32)],
        compiler_params=_params(1),
        name="ffn",
    )(h, g, wgu, wd)


def _ffn_ple_body(x_ref, g_ref, wgu_ref, wd_ref, p_ref, wple_ref, gple_ref, wgate_ref, gfin_ref,
                  o_ref, xn_ref, acc_ref, *, final):
    y = _swiglu(x_ref, g_ref, wgu_ref, wd_ref, xn_ref, acc_ref)
    h = x_ref[...] + 0.5 * y
    e = _rms(_dot(p_ref[...].astype(BF16), wple_ref[...]), gple_ref[...])
    h = h + jax.nn.sigmoid(_dot(h.astype(BF16), wgate_ref[...])) * e
    if final:
        h = _rms(h, gfin_ref[...])
    o_ref[...] = h


def _ffn_ple(h, layer, g, wgu, wd, p, wple, gple, wgate, gfin, *, final):
    n, d = h.shape
    row = pl.BlockSpec((FFN_ROWS, d), lambda i: (i, 0))
    return pl.pallas_call(
        functools.partial(_ffn_ple_body, final=final),
        out_shape=jax.ShapeDtypeStruct((n, d), F32),
        grid=(n // FFN_ROWS,),
        in_specs=[row, _layer_spec(g, layer), _layer_spec(wgu, layer), _layer_spec(wd, layer),
                  pl.BlockSpec((None, FFN_ROWS, p.shape[2]), lambda i: (layer, i, 0)),
                  _layer_spec(wple, layer), _layer_spec(gple, layer), _layer_spec(wgate, layer),
                  _layer_spec(gfin, 0)],
        out_specs=row,
        scratch_shapes=[pltpu.VMEM((FFN_ROWS, d), BF16), pltpu.VMEM((FFN_ROWS, d), F32)],
        compiler_params=_params(1),
        name="ffn_ple",
    )(h, g, wgu, wd, p, wple, gple, wgate, gfin)


def _in_proj_body(h_ref, g_ref, win_ref, cos_ref, sin_ref, q_ref, kx_ref, vx_ref, u_ref):
    xn = _rms(h_ref[...], g_ref[...]).astype(BF16)
    cos = cos_ref[...]
    sin = sin_ref[...]
    lane = lax.broadcasted_iota(jnp.int32, (1, LANES), 1)
    in_head = lane % HEAD_DIM
    half = ROT_DIM // 2
    s_lo = jnp.where(in_head < half, -sin, 0.0)
    s_hi = jnp.where((in_head >= half) & (in_head < ROT_DIM), sin, 0.0)

    def rope(t):
        return (t * cos + pltpu.roll(t, LANES - half, 1) * s_lo + pltpu.roll(t, half, 1) * s_hi)

    k_col = ATTN_W
    v_col = k_col + KV_W
    a_col = v_col + KV_W
    conv_ch = (win_ref.shape[1] - a_col) // 2

    scale = HEAD_DIM ** -0.5
    q = _dot(xn, win_ref[:, :k_col])
    for p in range(ATTN_W // LANES):
        cols = slice(p * LANES, (p + 1) * LANES)
        q_ref[:, cols] = (rope(q[:, cols]) * scale).astype(BF16)

    kv = _dot(xn, win_ref[:, k_col:a_col])
    low = lane < HEAD_DIM

    def head_pair_layout(t, out_ref):
        sw = pltpu.roll(t, HEAD_DIM, 1)
        out_ref[:, 0 * LANES:1 * LANES] = jnp.where(low, t, 0.0).astype(BF16)
        out_ref[:, 1 * LANES:2 * LANES] = jnp.where(low, 0.0, sw).astype(BF16)
        out_ref[:, 2 * LANES:3 * LANES] = jnp.where(low, sw, 0.0).astype(BF16)
        out_ref[:, 3 * LANES:4 * LANES] = jnp.where(low, 0.0, t).astype(BF16)

    head_pair_layout(rope(kv[:, :KV_W]), kx_ref)
    head_pair_layout(kv[:, KV_W:], vx_ref)

    ag = _dot(xn, win_ref[:, a_col:])
    u_ref[...] = ag[:, :conv_ch] * jax.nn.sigmoid(ag[:, conv_ch:])


def _in_proj(h, layer, g, win, cos, sin):
    n, d = h.shape
    rows = lambda w: pl.BlockSpec((PROJ_ROWS, w), lambda i: (i, 0))
    conv_ch = (win.shape[2] - ATTN_W - 2 * KV_W) // 2
    return pl.pallas_call(
        _in_proj_body,
        out_shape=(jax.ShapeDtypeStruct((n, ATTN_W), BF16),
                   jax.ShapeDtypeStruct((n, 4 * LANES), BF16),
                   jax.ShapeDtypeStruct((n, 4 * LANES), BF16),
                   jax.ShapeDtypeStruct((n, conv_ch), F32)),
        grid=(n // PROJ_ROWS,),
        in_specs=[rows(d), _layer_spec(g, layer), _layer_spec(win, layer), rows(LANES), rows(LANES)],
        out_specs=(rows(ATTN_W), rows(4 * LANES), rows(4 * LANES), rows(conv_ch)),
        compiler_params=_params(1),
        name="in_proj",
    )(h, g, win, cos, sin)


def _conv_rows(uall_ref, cw_ref, base):
    first_tap = CONV_HALO - (CONV_WIDTH - 1)
    y = None
    for r in range(SUBLANES):
        rows = CONV_SUB + (SUBLANES if r else 0)
        part = None
        for o in range(r, first_tap + CONV_WIDTH, SUBLANES):
            if o < first_tap:
                continue
            k = o - first_tap
            start = base + o - r
            term = cw_ref[k:k + 1, :] * uall_ref[start:start + rows, :]
            part = term if part is None else part + term
        part = part[r:r + CONV_SUB, :]
        y = part if y is None else y + part
    return y


def _mix_body(sinks_ref, h_ref, q_ref, kc_ref, kp_ref, vc_ref, vp_ref, uc_ref, up_ref,
              cw_ref, cb_ref, lng_ref, lnb_ref, an_ref, cn_ref, wout_ref,
              o_ref, kall_ref, vall_ref, uall_ref, mixed_ref, *, layer):
    i = pl.program_id(1)
    ts = h_ref.shape[0]
    not_first = i > 0

    kall_ref[:WINDOW] = kp_ref[...]
    kall_ref[WINDOW:] = kc_ref[...]
    vall_ref[:WINDOW] = vp_ref[...]
    vall_ref[WINDOW:] = vc_ref[...]
    uall_ref[:CONV_HALO] = jnp.where(not_first, up_ref[...], 0.0)
    uall_ref[CONV_HALO:] = uc_ref[...]

    r = lax.broadcasted_iota(jnp.int32, (WINDOW, 2 * WINDOW), 0)
    c = lax.broadcasted_iota(jnp.int32, (WINDOW, 2 * WINDOW), 1)
    band = (c > r) & (c <= r + WINDOW)
    band_first = band & ((c >= WINDOW) | not_first)

    def kv_tiles(p):
        grp = (2 * p) // GROUP
        return (slice((2 * grp) * LANES, (2 * grp + 1) * LANES),
                slice((2 * grp + 1) * LANES, (2 * grp + 2) * LANES))

    def softmax_sink(s, mask, sink):
        s = jnp.where(mask, s, MASK_VALUE)
        m = jnp.maximum(jnp.max(s, axis=-1, keepdims=True), sink)
        e = jnp.exp(s - m)
        den = jnp.sum(e, axis=-1, keepdims=True) + jnp.exp(sink - m)
        return (e * (1.0 / den)).astype(BF16)

    def conv_module(sb):
        base = sb * CONV_SUB
        y = _conv_rows(uall_ref, cw_ref, base) + cb_ref[...]
        mu = jnp.mean(y, axis=-1, keepdims=True)
        yc = y - mu
        yn = yc * lax.rsqrt(jnp.mean(yc * yc, axis=-1, keepdims=True) + EPS) * lng_ref[...] + lnb_ref[...]
        z = yn * jax.nn.sigmoid(yn)
        mixed_ref[base:base + CONV_SUB, ATTN_W:] = _rms(z, cn_ref[...]).astype(BF16)

    conv_per_block = WINDOW // CONV_SUB
    for qb in range(ts // WINDOW):
        rows = slice(qb * WINDOW, (qb + 1) * WINDOW)
        keys = slice(qb * WINDOW, (qb + 2) * WINDOW)
        mask = band_first if qb == 0 else band
        scores = []
        for p in range(N_HEADS // 2):
            lo, hi = kv_tiles(p)
            qp = q_ref[rows, p * LANES:(p + 1) * LANES]
            scores.append(_dot_nt(qp, kall_ref[keys, lo]))
            scores.append(_dot_nt(qp, kall_ref[keys, hi]))
        conv_module(qb * conv_per_block)
        probs = [softmax_sink(s, mask, sinks_ref[layer, hd]) for hd, s in enumerate(scores)]
        for sb in range(1, conv_per_block):
            conv_module(qb * conv_per_block + sb)
        outs = []
        for p in range(N_HEADS // 2):
            lo, hi = kv_tiles(p)
            outs.append(_dot(probs[2 * p], vall_ref[keys, lo]) + _dot(probs[2 * p + 1], vall_ref[keys, hi]))
        attn = jnp.concatenate(outs, axis=1)
        mixed_ref[rows, :ATTN_W] = _rms(attn, an_ref[...]).astype(BF16)

    o_ref[...] = h_ref[...] + _dot(mixed_ref[...], wout_ref[...])


def _mix(h, layer, q, kx, vx, u, sinks, cw, cb, lng, lnb, an, cn, wout):
    b, s, d = h.shape
    conv_ch = u.shape[2]
    ts = MIX_ROWS
    cur = lambda w: pl.BlockSpec((None, ts, w), lambda bi, i: (bi, i, 0))
    prev = lambda rows, w: pl.BlockSpec(
        (None, rows, w), lambda bi, i: (bi, jnp.maximum(i * (ts // rows) - 1, 0), 0))
    return pl.pallas_call(
        functools.partial(_mix_body, layer=layer),
        out_shape=jax.ShapeDtypeStruct((b, s, d), F32),
        grid=(b, s // ts),
        in_specs=[pl.BlockSpec(memory_space=pltpu.SMEM),
                  cur(d), cur(ATTN_W),
                  cur(4 * LANES), prev(WINDOW, 4 * LANES),
                  cur(4 * LANES), prev(WINDOW, 4 * LANES),
                  cur(conv_ch), prev(CONV_HALO, conv_ch),
                  _layer_spec(cw, layer), _layer_spec(cb, layer), _layer_spec(lng, layer),
                  _layer_spec(lnb, layer), _layer_spec(an, layer), _layer_spec(cn, layer),
                  _layer_spec(wout, layer)],
        out_specs=cur(d),
        scratch_shapes=[pltpu.VMEM((ts + WINDOW, 4 * LANES), BF16),
                        pltpu.VMEM((ts + WINDOW, 4 * LANES), BF16),
                        pltpu.VMEM((ts + CONV_HALO, conv_ch), F32),
                        pltpu.VMEM((ts, d), BF16)],
        compiler_params=_params(2),
        name="mix",
    )(sinks, h, q, kx, kx, vx, vx, u, u, cw, cb, lng, lnb, an, cn, wout)


def kernel(x, p, positions, ffn1_norm, ffn1_w_gate_up, ffn1_w_down, mix_norm, w_in, sinks, conv_w, conv_b, conv_ln_g, conv_ln_b, attn_out_norm, conv_out_norm, w_out, ffn2_norm, ffn2_w_gate_up, ffn2_w_down, w_ple, ple_norm, w_ple_gate, final_norm):
    b, s, d = x.shape
    depth = p.shape[0]
    n = b * s
    rows = lambda v: v.reshape(v.shape[0], 1, v.shape[1])
    bf = lambda w: w.astype(BF16)

    inv_freq = ROPE_THETA ** (-jnp.arange(0, ROT_DIM, 2, dtype=F32) / ROT_DIM)
    lane = jnp.arange(LANES) % HEAD_DIM
    invf_lane = jnp.where(lane < ROT_DIM, inv_freq[lane % (ROT_DIM // 2)], 0.0).reshape(1, LANES)
    cos, sin = _rope_tables(positions.astype(F32).reshape(n, 1), invf_lane)

    wgu1, wd1, wgu2, wd2 = bf(ffn1_w_gate_up), bf(ffn1_w_down), bf(ffn2_w_gate_up), bf(ffn2_w_down)
    win, wout, wple, wgate = bf(w_in), bf(w_out), bf(w_ple), bf(w_ple_gate)
    g1, gm, g2, gple = rows(ffn1_norm), rows(mix_norm), rows(ffn2_norm), rows(ple_norm)
    cb, lng, lnb = rows(conv_b), rows(conv_ln_g), rows(conv_ln_b)
    an, cn = rows(attn_out_norm), rows(conv_out_norm)
    gfin = final_norm.reshape(1, 1, d)
    p_rows = p.reshape(depth, n, p.shape[3])

    h = x.reshape(n, d)
    shp = lambda t: t.reshape(b, s, t.shape[1])
    for i in range(depth):
        h = _ffn(h, i, g1, wgu1, wd1)
        q, kx, vx, u = _in_proj(h, i, gm, win, cos, sin)
        h = _mix(shp(h), i, shp(q), shp(kx), shp(vx), shp(u), sinks, conv_w, cb, lng, lnb, an, cn,
                 wout).reshape(n, d)
        h = _ffn_ple(h, i, g2, wgu2, wd2, p_rows, wple, gple, wgate, gfin, final=(i == depth - 1))
    return h.reshape(b, s, d)
```

```python
import functools

import jax
import jax.numpy as jnp
from jax import lax
from jax.experimental import pallas as pl
from jax.experimental.pallas import tpu as pltpu

F32 = jnp.float32
BF16 = jnp.bfloat16

HEAD_DIM = 64
N_HEADS = 8
N_KV_HEADS = 2
GROUP = N_HEADS // N_KV_HEADS
ATTN_W = N_HEADS * HEAD_DIM
KV_W = N_KV_HEADS * HEAD_DIM
CONV_WIDTH = 31
WINDOW = 128
ROT_DIM = HEAD_DIM // 4
ROPE_THETA = 500000.0
EPS = 1e-6
MASK_VALUE = -1e30

LANES = 128
SUBLANES = 8

FFN_ROWS = 1024
FFN_CHUNK = 256
PROJ_ROWS = 1024
MIX_ROWS = 512
CONV_SUB = 64
CONV_HALO = 32
TABLE_ROWS = 2048
VMEM_LIMIT = 56 * 1024 * 1024


def _rms(x, g):
    return x * lax.rsqrt(jnp.mean(x * x, axis=-1, keepdims=True) + EPS) * g


def _dot(a, b):
    return jnp.dot(a, b, preferred_element_type=F32)


def _dot_nt(a, b):
    return lax.dot_general(a, b, (((1,), (1,)), ((), ())), preferred_element_type=F32)


def _layer_spec(arr, layer):
    tail = arr.shape[1:]
    idx = (layer,) + (0,) * len(tail)
    return pl.BlockSpec((None,) + tail, lambda *_: idx, pipeline_mode=pl.Buffered(1))


def _params(n_axes):
    return pltpu.CompilerParams(
        dimension_semantics=("arbitrary",) * n_axes, vmem_limit_bytes=VMEM_LIMIT)


def _rope_table_body(pos_ref, invf_ref, cos_ref, sin_ref):
    ang = pos_ref[...] * invf_ref[...]
    cos_ref[...] = jnp.cos(ang)
    sin_ref[...] = jnp.sin(ang)


def _rope_tables(pos, invf_lane):
    n = pos.shape[0]
    out = jax.ShapeDtypeStruct((n, LANES), F32)
    return pl.pallas_call(
        _rope_table_body,
        out_shape=(out, out),
        grid=(n // TABLE_ROWS,),
        in_specs=[pl.BlockSpec((TABLE_ROWS, 1), lambda i: (i, 0)),
                  pl.BlockSpec((1, LANES), lambda i: (0, 0))],
        out_specs=(pl.BlockSpec((TABLE_ROWS, LANES), lambda i: (i, 0)),
                   pl.BlockSpec((TABLE_ROWS, LANES), lambda i: (i, 0))),
        compiler_params=_params(1),
        name="rope_table",
    )(pos, invf_lane)


def _swiglu_chunk(xn_ref, wgu_ref, wd_ref, c):
    ff = wd_ref.shape[0]
    lo = c * FFN_CHUNK
    xn = xn_ref[...]
    g = _dot(xn, wgu_ref[:, lo:lo + FFN_CHUNK])
    u = _dot(xn, wgu_ref[:, ff + lo:ff + lo + FFN_CHUNK])
    a = (g * jax.nn.sigmoid(g) * u).astype(BF16)
    return _dot(a, wd_ref[lo:lo + FFN_CHUNK, :])


def _swiglu(xn_ref, wgu_ref, wd_ref, acc_ref, units=None):
    n_chunks = wd_ref.shape[0] // FFN_CHUNK
    y = None
    for c in range(n_chunks):
        d = _swiglu_chunk(xn_ref, wgu_ref, wd_ref, c)
        if c == 0:
            acc_ref[...] = d
        elif c < n_chunks - 1:
            acc_ref[...] += d
        else:
            y = acc_ref[...] + d
        for thunk in (units[c] if units else ()):
            thunk()
    return y


def _ffn_body(x_ref, g_ref, wgu_ref, wd_ref, o_ref, xn_ref, acc_ref):
    xn_ref[...] = _rms(x_ref[...], g_ref[...]).astype(BF16)
    y = _swiglu(xn_ref, wgu_ref, wd_ref, acc_ref)
    o_ref[...] = x_ref[...] + 0.5 * y


def _ffn(h, layer, g, wgu, wd):
    n, d = h.shape
    row = pl.BlockSpec((FFN_ROWS, d), lambda i: (i, 0))
    return pl.pallas_call(
        _ffn_body,
        out_shape=jax.ShapeDtypeStruct((n, d), F32),
        grid=(n // FFN_ROWS,),
        in_specs=[row, _layer_spec(g, layer), _layer_spec(wgu, layer), _layer_spec(wd, layer)],
        out_specs=row,
        scratch_shapes=[pltpu.VMEM((FFN_ROWS, d), BF16), pltpu.VMEM((FFN_ROWS, d), F32)],
        compiler_params=_params(1),
        name="ffn",
    )(h, g, wgu, wd)


def _in_proj_body(h_ref, g_ref, win_ref, cos_ref, sin_ref, q_ref, kx_ref, vx_ref, u_ref):
    xn = _rms(h_ref[...], g_ref[...]).astype(BF16)
    cos = cos_ref[...]
    sin = sin_ref[...]
    lane = lax.broadcasted_iota(jnp.int32, (1, LANES), 1)
    in_head = lane % HEAD_DIM
    half = ROT_DIM // 2
    s_lo = jnp.where(in_head < half, -sin, 0.0)
    s_hi = jnp.where((in_head >= half) & (in_head < ROT_DIM), sin, 0.0)

    def rope(t):
        return (t * cos + pltpu.roll(t, LANES - half, 1) * s_lo + pltpu.roll(t, half, 1) * s_hi)

    k_col = ATTN_W
    v_col = k_col + KV_W
    a_col = v_col + KV_W
    conv_ch = (win_ref.shape[1] - a_col) // 2

    scale = HEAD_DIM ** -0.5
    q = _dot(xn, win_ref[:, :k_col])
    for p in range(ATTN_W // LANES):
        cols = slice(p * LANES, (p + 1) * LANES)
        q_ref[:, cols] = (rope(q[:, cols]) * scale).astype(BF16)

    kv = _dot(xn, win_ref[:, k_col:a_col])
    low = lane < HEAD_DIM

    def head_pair_layout(t, out_ref):
        sw = pltpu.roll(t, HEAD_DIM, 1)
        out_ref[:, 0 * LANES:1 * LANES] = jnp.where(low, t, 0.0).astype(BF16)
        out_ref[:, 1 * LANES:2 * LANES] = jnp.where(low, 0.0, sw).astype(BF16)
        out_ref[:, 2 * LANES:3 * LANES] = jnp.where(low, sw, 0.0).astype(BF16)
        out_ref[:, 3 * LANES:4 * LANES] = jnp.where(low, 0.0, t).astype(BF16)

    head_pair_layout(rope(kv[:, :KV_W]), kx_ref)
    head_pair_layout(kv[:, KV_W:], vx_ref)

    ag = _dot(xn, win_ref[:, a_col:])
    u_ref[...] = ag[:, :conv_ch] * jax.nn.sigmoid(ag[:, conv_ch:])


def _in_proj(h, layer, g, win, cos, sin):
    n, d = h.shape
    rows = lambda w: pl.BlockSpec((PROJ_ROWS, w), lambda i: (i, 0))
    conv_ch = (win.shape[2] - ATTN_W - 2 * KV_W) // 2
    return pl.pallas_call(
        _in_proj_body,
        out_shape=(jax.ShapeDtypeStruct((n, ATTN_W), BF16),
                   jax.ShapeDtypeStruct((n, 4 * LANES), BF16),
                   jax.ShapeDtypeStruct((n, 4 * LANES), BF16),
                   jax.ShapeDtypeStruct((n, conv_ch), F32)),
        grid=(n // PROJ_ROWS,),
        in_specs=[rows(d), _layer_spec(g, layer), _layer_spec(win, layer), rows(LANES), rows(LANES)],
        out_specs=(rows(ATTN_W), rows(4 * LANES), rows(4 * LANES), rows(conv_ch)),
        compiler_params=_params(1),
        name="in_proj",
    )(h, g, win, cos, sin)


def _conv_rows(uall_ref, cw_ref, base):
    first_tap = CONV_HALO - (CONV_WIDTH - 1)
    y = None
    for r in range(SUBLANES):
        rows = CONV_SUB + (SUBLANES if r else 0)
        part = None
        for o in range(r, first_tap + CONV_WIDTH, SUBLANES):
            if o < first_tap:
                continue
            k = o - first_tap
            start = base + o - r
            term = cw_ref[k:k + 1, :] * uall_ref[start:start + rows, :]
            part = term if part is None else part + term
        part = part[r:r + CONV_SUB, :]
        y = part if y is None else y + part
    return y


def _mix_ffn_body(sinks_ref, h_ref, q_ref, kc_ref, kp_ref, vc_ref, vp_ref, uc_ref, up_ref,
                  cw_ref, cb_ref, lng_ref, lnb_ref, an_ref, cn_ref, wout_ref,
                  g2_ref, wgu_ref, wd_ref, p_ref, wple_ref, gple_ref, wgate_ref, gfin_ref,
                  o_ref, kall_ref, vall_ref, uall_ref, mixed_ref, hmid_ref, xn_ref, acc_ref,
                  *, layer, tiles_per_seq, n_tiles, final):
    t = pl.program_id(0)
    ts = h_ref.shape[0]
    not_first = jnp.minimum(t, n_tiles - 1) % tiles_per_seq > 0

    @pl.when(t == 0)
    def _():
        hmid_ref[...] = jnp.zeros_like(hmid_ref)

    h_prev = hmid_ref[...]
    o_ref[...] = h_prev
    xn_ref[...] = _rms(h_prev, g2_ref[...]).astype(BF16)

    kall_ref[:WINDOW] = kp_ref[...]
    kall_ref[WINDOW:] = kc_ref[...]
    vall_ref[:WINDOW] = vp_ref[...]
    vall_ref[WINDOW:] = vc_ref[...]
    uall_ref[:CONV_HALO] = jnp.where(not_first, up_ref[...], 0.0)
    uall_ref[CONV_HALO:] = uc_ref[...]

    r = lax.broadcasted_iota(jnp.int32, (WINDOW, 2 * WINDOW), 0)
    c = lax.broadcasted_iota(jnp.int32, (WINDOW, 2 * WINDOW), 1)
    band = (c > r) & (c <= r + WINDOW)
    band_first = band & ((c >= WINDOW) | not_first)

    def kv_tiles(p):
        grp = (2 * p) // GROUP
        return (slice((2 * grp) * LANES, (2 * grp + 1) * LANES),
                slice((2 * grp + 1) * LANES, (2 * grp + 2) * LANES))

    def softmax_sink(s, mask, sink):
        s = jnp.where(mask, s, MASK_VALUE)
        m = jnp.maximum(jnp.max(s, axis=-1, keepdims=True), sink)
        e = jnp.exp(s - m)
        den = jnp.sum(e, axis=-1, keepdims=True) + jnp.exp(sink - m)
        return (e * (1.0 / den)).astype(BF16)

    def conv_module(sb):
        base = sb * CONV_SUB
        y = _conv_rows(uall_ref, cw_ref, base) + cb_ref[...]
        mu = jnp.mean(y, axis=-1, keepdims=True)
        yc = y - mu
        yn = yc * lax.rsqrt(jnp.mean(yc * yc, axis=-1, keepdims=True) + EPS) * lng_ref[...] + lnb_ref[...]
        z = yn * jax.nn.sigmoid(yn)
        out = _rms(z, cn_ref[...]).astype(BF16)
        mixed_ref[base:base + CONV_SUB, ATTN_W:] = out

    conv_per_block = WINDOW // CONV_SUB
    stages = []
    for qb in range(ts // WINDOW):
        rows = slice(qb * WINDOW, (qb + 1) * WINDOW)
        keys = slice(qb * WINDOW, (qb + 2) * WINDOW)
        mask = band_first if qb == 0 else band
        state = {}

        def scores_stage(rows=rows, keys=keys, state=state):
            scores = []
            for p in range(N_HEADS // 2):
                lo, hi = kv_tiles(p)
                qp = q_ref[rows, p * LANES:(p + 1) * LANES]
                scores.append(_dot_nt(qp, kall_ref[keys, lo]))
                scores.append(_dot_nt(qp, kall_ref[keys, hi]))
            state["scores"] = scores

        def softmax_stage(mask=mask, state=state):
            state["probs"] = [softmax_sink(s, mask, sinks_ref[layer, hd])
                              for hd, s in enumerate(state.pop("scores"))]

        def values_stage(rows=rows, keys=keys, state=state):
            probs = state.pop("probs")
            outs = []
            for p in range(N_HEADS // 2):
                lo, hi = kv_tiles(p)
                outs.append(_dot(probs[2 * p], vall_ref[keys, lo])
                            + _dot(probs[2 * p + 1], vall_ref[keys, hi]))
            mixed_ref[rows, :ATTN_W] = _rms(jnp.concatenate(outs, axis=1), an_ref[...]).astype(BF16)

        conv_stages = [functools.partial(conv_module, qb * conv_per_block + sb)
                       for sb in range(conv_per_block)]
        stages += [scores_stage, conv_stages[0], softmax_stage, *conv_stages[1:], values_stage]

    n_chunks = wd_ref.shape[0] // FFN_CHUNK
    per_chunk = -(-len(stages) // n_chunks)
    units = [stages[c * per_chunk:(c + 1) * per_chunk] for c in range(n_chunks)]
    y = _swiglu(xn_ref, wgu_ref, wd_ref, acc_ref, units)

    hmid_ref[...] = h_ref[...] + _dot(mixed_ref[...], wout_ref[...])

    h = o_ref[...] + 0.5 * y
    e = _rms(_dot(p_ref[...].astype(BF16), wple_ref[...]), gple_ref[...])
    h = h + jax.nn.sigmoid(_dot(h.astype(BF16), wgate_ref[...])) * e
    if final:
        h = _rms(h, gfin_ref[...])
    o_ref[...] = h


def _mix_ffn(h, layer, q, kx, vx, u, sinks, cw, cb, lng, lnb, an, cn, wout,
             g2, wgu, wd, p, wple, gple, wgate, gfin, *, seq_len, final):
    n, d = h.shape
    conv_ch = u.shape[1]
    ts = MIX_ROWS
    n_tiles = n // ts
    mix_tile = lambda t: jnp.minimum(t, n_tiles - 1)
    ffn_tile = lambda t: jnp.maximum(t - 1, 0)
    cur = lambda w: pl.BlockSpec((ts, w), lambda t: (mix_tile(t), 0))
    prev = lambda rows, w: pl.BlockSpec(
        (rows, w), lambda t: (jnp.maximum(mix_tile(t) * (ts // rows) - 1, 0), 0))
    return pl.pallas_call(
        functools.partial(_mix_ffn_body, layer=layer, tiles_per_seq=seq_len // ts, n_tiles=n_tiles,
                          final=final),
        out_shape=jax.ShapeDtypeStruct((n, d), F32),
        grid=(n_tiles + 1,),
        in_specs=[pl.BlockSpec(memory_space=pltpu.SMEM),
                  cur(d), cur(ATTN_W),
                  cur(4 * LANES), prev(WINDOW, 4 * LANES),
                  cur(4 * LANES), prev(WINDOW, 4 * LANES),
                  cur(conv_ch), prev(CONV_HALO, conv_ch),
                  _layer_spec(cw, layer), _layer_spec(cb, layer), _layer_spec(lng, layer),
                  _layer_spec(lnb, layer), _layer_spec(an, layer), _layer_spec(cn, layer),
                  _layer_spec(wout, layer),
                  _layer_spec(g2, layer), _layer_spec(wgu, layer), _layer_spec(wd, layer),
                  pl.BlockSpec((None, ts, p.shape[2]), lambda t: (layer, ffn_tile(t), 0)),
                  _layer_spec(wple, layer), _layer_spec(gple, layer), _layer_spec(wgate, layer),
                  _layer_spec(gfin, 0)],
        out_specs=pl.BlockSpec((ts, d), lambda t: (ffn_tile(t), 0)),
        scratch_shapes=[pltpu.VMEM((ts + WINDOW, 4 * LANES), BF16),
                        pltpu.VMEM((ts + WINDOW, 4 * LANES), BF16),
                        pltpu.VMEM((ts + CONV_HALO, conv_ch), F32),
                        pltpu.VMEM((ts, d), BF16),
                        pltpu.VMEM((ts, d), F32),
                        pltpu.VMEM((ts, d), BF16),
                        pltpu.VMEM((ts, d), F32)],
        compiler_params=_params(1),
        name="mix_ffn",
    )(sinks, h, q, kx, kx, vx, vx, u, u, cw, cb, lng, lnb, an, cn, wout,
      g2, wgu, wd, p, wple, gple, wgate, gfin)


def kernel(x, p, positions, ffn1_norm, ffn1_w_gate_up, ffn1_w_down, mix_norm, w_in, sinks, conv_w, conv_b, conv_ln_g, conv_ln_b, attn_out_norm, conv_out_norm, w_out, ffn2_norm, ffn2_w_gate_up, ffn2_w_down, w_ple, ple_norm, w_ple_gate, final_norm):
    b, s, d = x.shape
    depth = p.shape[0]
    n = b * s
    rows = lambda v: v.reshape(v.shape[0], 1, v.shape[1])
    bf = lambda w: w.astype(BF16)

    inv_freq = ROPE_THETA ** (-jnp.arange(0, ROT_DIM, 2, dtype=F32) / ROT_DIM)
    lane = jnp.arange(LANES) % HEAD_DIM
    invf_lane = jnp.where(lane < ROT_DIM, inv_freq[lane % (ROT_DIM // 2)], 0.0).reshape(1, LANES)
    cos, sin = _rope_tables(positions.astype(F32).reshape(n, 1), invf_lane)

    wgu1, wd1, wgu2, wd2 = bf(ffn1_w_gate_up), bf(ffn1_w_down), bf(ffn2_w_gate_up), bf(ffn2_w_down)
    win, wout, wple, wgate = bf(w_in), bf(w_out), bf(w_ple), bf(w_ple_gate)
    g1, gm, g2, gple = rows(ffn1_norm), rows(mix_norm), rows(ffn2_norm), rows(ple_norm)
    cb, lng, lnb = rows(conv_b), rows(conv_ln_g), rows(conv_ln_b)
    an, cn = rows(attn_out_norm), rows(conv_out_norm)
    gfin = final_norm.reshape(1, 1, d)
    p_rows = p.reshape(depth, n, p.shape[3])

    h = x.reshape(n, d)
    for i in range(depth):
        h = _ffn(h, i, g1, wgu1, wd1)
        q, kx, vx, u = _in_proj(h, i, gm, win, cos, sin)
        h = _mix_ffn(h, i, q, kx, vx, u, sinks, conv_w, cb, lng, lnb, an, cn, wout,
                     g2, wgu2, wd2, p_rows, wple, gple, wgate, gfin,
                     seq_len=s, final=(i == depth - 1))
    return h.reshape(b, s, d)
```

```python
import functools

import jax
import jax.numpy as jnp
from jax import lax
from jax.experimental import pallas as pl
from jax.experimental.pallas import tpu as pltpu

F32 = jnp.float32
BF16 = jnp.bfloat16

HEAD_DIM = 64
N_HEADS = 8
N_KV_HEADS = 2
GROUP = N_HEADS // N_KV_HEADS
ATTN_W = N_HEADS * HEAD_DIM
KV_W = N_KV_HEADS * HEAD_DIM
CONV_WIDTH = 31
WINDOW = 128
ROT_DIM = HEAD_DIM // 4
ROPE_THETA = 500000.0
EPS = 1e-6
MASK_VALUE = -1e30

LANES = 128
SUBLANES = 8

FFN_ROWS = 1024
FFN_CHUNK = 256
PROJ_ROWS = 1024
MIX_ROWS = 512
CONV_SUB = 64
CONV_HALO = 32
TABLE_ROWS = 2048
VMEM_LIMIT = 56 * 1024 * 1024


def _rms(x, g):
    return x * lax.rsqrt(jnp.mean(x * x, axis=-1, keepdims=True) + EPS) * g


def _dot(a, b):
    return jnp.dot(a, b, preferred_element_type=F32)


def _dot_nt(a, b):
    return lax.dot_general(a, b, (((1,), (1,)), ((), ())), preferred_element_type=F32)


def _layer_spec(arr, layer):
    tail = arr.shape[1:]
    idx = (layer,) + (0,) * len(tail)
    return pl.BlockSpec((None,) + tail, lambda *_: idx, pipeline_mode=pl.Buffered(1))


def _params(n_axes):
    return pltpu.CompilerParams(
        dimension_semantics=("arbitrary",) * n_axes, vmem_limit_bytes=VMEM_LIMIT)


def _rope_table_body(pos_ref, invf_ref, cos_ref, sin_ref):
    ang = pos_ref[...] * invf_ref[...]
    cos_ref[...] = jnp.cos(ang)
    sin_ref[...] = jnp.sin(ang)


def _rope_tables(pos, invf_lane):
    n = pos.shape[0]
    out = jax.ShapeDtypeStruct((n, LANES), F32)
    return pl.pallas_call(
        _rope_table_body,
        out_shape=(out, out),
        grid=(n // TABLE_ROWS,),
        in_specs=[pl.BlockSpec((TABLE_ROWS, 1), lambda i: (i, 0)),
                  pl.BlockSpec((1, LANES), lambda i: (0, 0))],
        out_specs=(pl.BlockSpec((TABLE_ROWS, LANES), lambda i: (i, 0)),
                   pl.BlockSpec((TABLE_ROWS, LANES), lambda i: (i, 0))),
        compiler_params=_params(1),
        name="rope_table",
    )(pos, invf_lane)


def _swiglu_chunk(xn_ref, wgu_ref, wd_ref, c):
    ff = wd_ref.shape[0]
    lo = c * FFN_CHUNK
    xn = xn_ref[...]
    g = _dot(xn, wgu_ref[:, lo:lo + FFN_CHUNK])
    u = _dot(xn, wgu_ref[:, ff + lo:ff + lo + FFN_CHUNK])
    a = (g * jax.nn.sigmoid(g) * u).astype(BF16)
    return _dot(a, wd_ref[lo:lo + FFN_CHUNK, :])


ANCHOR_ROWS = 16


def _anchor(ref, row0, value, always):
    rows = slice(row0, row0 + ANCHOR_ROWS)
    ref[rows, :LANES] = jnp.where(always, ref[rows, :LANES], value[:ANCHOR_ROWS, :LANES].astype(ref.dtype))


def _swiglu(xn_ref, wgu_ref, wd_ref, acc_ref, units=None, always=None):
    n_chunks = wd_ref.shape[0] // FFN_CHUNK
    y = None
    d = None
    for c in range(n_chunks):
        if always is not None and d is not None:
            _anchor(xn_ref, 0, d, always)
        d = _swiglu_chunk(xn_ref, wgu_ref, wd_ref, c)
        if c == 0:
            acc_ref[...] = d
        elif c < n_chunks - 1:
            acc_ref[...] += d
        else:
            y = acc_ref[...] + d
        for thunk in (units[c] if units else ()):
            thunk()
    return y


def _ffn_body(x_ref, g_ref, wgu_ref, wd_ref, o_ref, xn_ref, acc_ref):
    xn_ref[...] = _rms(x_ref[...], g_ref[...]).astype(BF16)
    y = _swiglu(xn_ref, wgu_ref, wd_ref, acc_ref)
    o_ref[...] = x_ref[...] + 0.5 * y


def _ffn(h, layer, g, wgu, wd):
    n, d = h.shape
    row = pl.BlockSpec((FFN_ROWS, d), lambda i: (i, 0))
    return pl.pallas_call(
        _ffn_body,
        out_shape=jax.ShapeDtypeStruct((n, d), F32),
        grid=(n // FFN_ROWS,),
        in_specs=[row, _layer_spec(g, layer), _layer_spec(wgu, layer), _layer_spec(wd, layer)],
        out_specs=row,
        scratch_shapes=[pltpu.VMEM((FFN_ROWS, d), BF16), pltpu.VMEM((FFN_ROWS, d), F32)],
        compiler_params=_params(1),
        name="ffn",
    )(h, g, wgu, wd)


def _in_proj_body(h_ref, g_ref, win_ref, cos_ref, sin_ref, q_ref, kx_ref, vx_ref, u_ref):
    xn = _rms(h_ref[...], g_ref[...]).astype(BF16)
    cos = cos_ref[...]
    sin = sin_ref[...]
    lane = lax.broadcasted_iota(jnp.int32, (1, LANES), 1)
    in_head = lane % HEAD_DIM
    half = ROT_DIM // 2
    s_lo = jnp.where(in_head < half, -sin, 0.0)
    s_hi = jnp.where((in_head >= half) & (in_head < ROT_DIM), sin, 0.0)

    def rope(t):
        return (t * cos + pltpu.roll(t, LANES - half, 1) * s_lo + pltpu.roll(t, half, 1) * s_hi)

    k_col = ATTN_W
    v_col = k_col + KV_W
    a_col = v_col + KV_W
    conv_ch = (win_ref.shape[1] - a_col) // 2

    scale = HEAD_DIM ** -0.5
    q = _dot(xn, win_ref[:, :k_col])
    for p in range(ATTN_W // LANES):
        cols = slice(p * LANES, (p + 1) * LANES)
        q_ref[:, cols] = (rope(q[:, cols]) * scale).astype(BF16)

    kv = _dot(xn, win_ref[:, k_col:a_col])
    low = lane < HEAD_DIM

    def head_pair_layout(t, out_ref):
        sw = pltpu.roll(t, HEAD_DIM, 1)
        out_ref[:, 0 * LANES:1 * LANES] = jnp.where(low, t, 0.0).astype(BF16)
        out_ref[:, 1 * LANES:2 * LANES] = jnp.where(low, 0.0, sw).astype(BF16)
        out_ref[:, 2 * LANES:3 * LANES] = jnp.where(low, sw, 0.0).astype(BF16)
        out_ref[:, 3 * LANES:4 * LANES] = jnp.where(low, 0.0, t).astype(BF16)

    head_pair_layout(rope(kv[:, :KV_W]), kx_ref)
    head_pair_layout(kv[:, KV_W:], vx_ref)

    ag = _dot(xn, win_ref[:, a_col:])
    u_ref[...] = ag[:, :conv_ch] * jax.nn.sigmoid(ag[:, conv_ch:])


def _in_proj(h, layer, g, win, cos, sin):
    n, d = h.shape
    rows = lambda w: pl.BlockSpec((PROJ_ROWS, w), lambda i: (i, 0))
    conv_ch = (win.shape[2] - ATTN_W - 2 * KV_W) // 2
    return pl.pallas_call(
        _in_proj_body,
        out_shape=(jax.ShapeDtypeStruct((n, ATTN_W), BF16),
                   jax.ShapeDtypeStruct((n, 4 * LANES), BF16),
                   jax.ShapeDtypeStruct((n, 4 * LANES), BF16),
                   jax.ShapeDtypeStruct((n, conv_ch), F32)),
        grid=(n // PROJ_ROWS,),
        in_specs=[rows(d), _layer_spec(g, layer), _layer_spec(win, layer), rows(LANES), rows(LANES)],
        out_specs=(rows(ATTN_W), rows(4 * LANES), rows(4 * LANES), rows(conv_ch)),
        compiler_params=_params(1),
        name="in_proj",
    )(h, g, win, cos, sin)


def _conv_rows(uall_ref, cw_ref, base):
    first_tap = CONV_HALO - (CONV_WIDTH - 1)
    tiles = []
    for j in range(uall_ref.shape[0]):
        lanes = slice(j * LANES, (j + 1) * LANES)
        acc = None
        for k in range(CONV_WIDTH):
            start = base + first_tap + k
            term = cw_ref[k:k + 1, lanes] * uall_ref[j, start:start + CONV_SUB, :]
            acc = term if acc is None else acc + term
        tiles.append(acc)
    return jnp.concatenate(tiles, axis=1)


def _mix_ffn_body(sinks_ref, h_ref, q_ref, kc_ref, kp_ref, vc_ref, vp_ref, uc_ref, up_ref,
                  cw_ref, cb_ref, lng_ref, lnb_ref, an_ref, cn_ref, wout_ref,
                  g2_ref, wgu_ref, wd_ref, p_ref, wple_ref, gple_ref, wgate_ref, gfin_ref,
                  o_ref, kall_ref, vall_ref, uall_ref, mixed_ref, hmid_ref, xn_ref, acc_ref,
                  *, layer, tiles_per_seq, n_tiles, final):
    t = pl.program_id(0)
    ts = h_ref.shape[0]
    not_first = jnp.minimum(t, n_tiles - 1) % tiles_per_seq > 0

    @pl.when(t == 0)
    def _():
        hmid_ref[...] = jnp.zeros_like(hmid_ref)

    h_prev = hmid_ref[...]
    o_ref[...] = h_prev
    xn_ref[...] = _rms(h_prev, g2_ref[...]).astype(BF16)

    kall_ref[:WINDOW] = kp_ref[...]
    kall_ref[WINDOW:] = kc_ref[...]
    vall_ref[:WINDOW] = vp_ref[...]
    vall_ref[WINDOW:] = vc_ref[...]
    for j in range(uall_ref.shape[0]):
        lanes = slice(j * LANES, (j + 1) * LANES)
        uall_ref[j, :CONV_HALO, :] = jnp.where(not_first, up_ref[:, lanes], 0.0)
        uall_ref[j, CONV_HALO:, :] = uc_ref[:, lanes]

    r = lax.broadcasted_iota(jnp.int32, (WINDOW, 2 * WINDOW), 0)
    c = lax.broadcasted_iota(jnp.int32, (WINDOW, 2 * WINDOW), 1)
    band = (c > r) & (c <= r + WINDOW)
    band_first = band & ((c >= WINDOW) | not_first)

    def kv_tiles(p):
        grp = (2 * p) // GROUP
        return (slice((2 * grp) * LANES, (2 * grp + 1) * LANES),
                slice((2 * grp + 1) * LANES, (2 * grp + 2) * LANES))

    def softmax_sink(s, mask, sink):
        s = jnp.where(mask, s, MASK_VALUE)
        m = jnp.maximum(jnp.max(s, axis=-1, keepdims=True), sink)
        e = jnp.exp(s - m)
        den = jnp.sum(e, axis=-1, keepdims=True) + jnp.exp(sink - m)
        return (e * (1.0 / den)).astype(BF16)

    always = t >= 0
    chain = {}

    def chain_after(ref, row0):
        if "token" in chain:
            _anchor(ref, row0, chain["token"], always)

    def conv_module(sb):
        base = sb * CONV_SUB
        chain_after(uall_ref.at[0], base)
        y = _conv_rows(uall_ref, cw_ref, base) + cb_ref[...]
        chain["token"] = y
        mu = jnp.mean(y, axis=-1, keepdims=True)
        yc = y - mu
        yn = yc * lax.rsqrt(jnp.mean(yc * yc, axis=-1, keepdims=True) + EPS) * lng_ref[...] + lnb_ref[...]
        z = yn * jax.nn.sigmoid(yn)
        out = _rms(z, cn_ref[...]).astype(BF16)
        mixed_ref[base:base + CONV_SUB, ATTN_W:] = out

    conv_per_block = WINDOW // CONV_SUB
    stages = []
    for qb in range(ts // WINDOW):
        rows = slice(qb * WINDOW, (qb + 1) * WINDOW)
        keys = slice(qb * WINDOW, (qb + 2) * WINDOW)
        mask = band_first if qb == 0 else band
        state = {}

        def scores_stage(rows=rows, keys=keys, state=state):
            chain_after(kall_ref, keys.start)
            scores = []
            for p in range(N_HEADS // 2):
                lo, hi = kv_tiles(p)
                qp = q_ref[rows, p * LANES:(p + 1) * LANES]
                scores.append(_dot_nt(qp, kall_ref[keys, lo]))
                scores.append(_dot_nt(qp, kall_ref[keys, hi]))
            state["scores"] = scores
            chain["token"] = scores[0]

        def softmax_stage(mask=mask, state=state):
            state["probs"] = [softmax_sink(s, mask, sinks_ref[layer, hd])
                              for hd, s in enumerate(state.pop("scores"))]
            chain["token"] = state["probs"][0]

        def values_stage(rows=rows, keys=keys, state=state):
            chain_after(vall_ref, keys.start)
            probs = state.pop("probs")
            outs = []
            for p in range(N_HEADS // 2):
                lo, hi = kv_tiles(p)
                outs.append(_dot(probs[2 * p], vall_ref[keys, lo])
                            + _dot(probs[2 * p + 1], vall_ref[keys, hi]))
            mixed_ref[rows, :ATTN_W] = _rms(jnp.concatenate(outs, axis=1), an_ref[...]).astype(BF16)
            chain["token"] = outs[0]

        conv_stages = [functools.partial(conv_module, qb * conv_per_block + sb)
                       for sb in range(conv_per_block)]
        stages += [scores_stage, conv_stages[0], softmax_stage, *conv_stages[1:], values_stage]

    n_chunks = wd_ref.shape[0] // FFN_CHUNK
    per_chunk = -(-len(stages) // n_chunks)
    units = [stages[c * per_chunk:(c + 1) * per_chunk] for c in range(n_chunks)]
    y = _swiglu(xn_ref, wgu_ref, wd_ref, acc_ref, units, always)

    hmid_ref[...] = h_ref[...] + _dot(mixed_ref[...], wout_ref[...])

    h = o_ref[...] + 0.5 * y
    e = _rms(_dot(p_ref[...].astype(BF16), wple_ref[...]), gple_ref[...])
    h = h + jax.nn.sigmoid(_dot(h.astype(BF16), wgate_ref[...])) * e
    if final:
        h = _rms(h, gfin_ref[...])
    o_ref[...] = h


def _mix_ffn(h, layer, q, kx, vx, u, sinks, cw, cb, lng, lnb, an, cn, wout,
             g2, wgu, wd, p, wple, gple, wgate, gfin, *, seq_len, final):
    n, d = h.shape
    conv_ch = u.shape[1]
    ts = MIX_ROWS
    n_tiles = n // ts
    mix_tile = lambda t: jnp.minimum(t, n_tiles - 1)
    ffn_tile = lambda t: jnp.maximum(t - 1, 0)
    cur = lambda w: pl.BlockSpec((ts, w), lambda t: (mix_tile(t), 0))
    prev = lambda rows, w: pl.BlockSpec(
        (rows, w), lambda t: (jnp.maximum(mix_tile(t) * (ts // rows) - 1, 0), 0))
    return pl.pallas_call(
        functools.partial(_mix_ffn_body, layer=layer, tiles_per_seq=seq_len // ts, n_tiles=n_tiles,
                          final=final),
        out_shape=jax.ShapeDtypeStruct((n, d), F32),
        grid=(n_tiles + 1,),
        in_specs=[pl.BlockSpec(memory_space=pltpu.SMEM),
                  cur(d), cur(ATTN_W),
                  cur(4 * LANES), prev(WINDOW, 4 * LANES),
                  cur(4 * LANES), prev(WINDOW, 4 * LANES),
                  cur(conv_ch), prev(CONV_HALO, conv_ch),
                  _layer_spec(cw, layer), _layer_spec(cb, layer), _layer_spec(lng, layer),
                  _layer_spec(lnb, layer), _layer_spec(an, layer), _layer_spec(cn, layer),
                  _layer_spec(wout, layer),
                  _layer_spec(g2, layer), _layer_spec(wgu, layer), _layer_spec(wd, layer),
                  pl.BlockSpec((None, ts, p.shape[2]), lambda t: (layer, ffn_tile(t), 0)),
                  _layer_spec(wple, layer), _layer_spec(gple, layer), _layer_spec(wgate, layer),
                  _layer_spec(gfin, 0)],
        out_specs=pl.BlockSpec((ts, d), lambda t: (ffn_tile(t), 0)),
        scratch_shapes=[pltpu.VMEM((ts + WINDOW, 4 * LANES), BF16),
                        pltpu.VMEM((ts + WINDOW, 4 * LANES), BF16),
                        pltpu.VMEM((conv_ch // LANES, ts + CONV_HALO, LANES), F32),
                        pltpu.VMEM((ts, d), BF16),
                        pltpu.VMEM((ts, d), F32),
                        pltpu.VMEM((ts, d), BF16),
                        pltpu.VMEM((ts, d), F32)],
        compiler_params=_params(1),
        name="mix_ffn",
    )(sinks, h, q, kx, kx, vx, vx, u, u, cw, cb, lng, lnb, an, cn, wout,
      g2, wgu, wd, p, wple, gple, wgate, gfin)


def kernel(x, p, positions, ffn1_norm, ffn1_w_gate_up, ffn1_w_down, mix_norm, w_in, sinks, conv_w, conv_b, conv_ln_g, conv_ln_b, attn_out_norm, conv_out_norm, w_out, ffn2_norm, ffn2_w_gate_up, ffn2_w_down, w_ple, ple_norm, w_ple_gate, final_norm):
    b, s, d = x.shape
    depth = p.shape[0]
    n = b * s
    rows = lambda v: v.reshape(v.shape[0], 1, v.shape[1])
    bf = lambda w: w.astype(BF16)

    inv_freq = ROPE_THETA ** (-jnp.arange(0, ROT_DIM, 2, dtype=F32) / ROT_DIM)
    lane = jnp.arange(LANES) % HEAD_DIM
    invf_lane = jnp.where(lane < ROT_DIM, inv_freq[lane % (ROT_DIM // 2)], 0.0).reshape(1, LANES)
    cos, sin = _rope_tables(positions.astype(F32).reshape(n, 1), invf_lane)

    wgu1, wd1, wgu2, wd2 = bf(ffn1_w_gate_up), bf(ffn1_w_down), bf(ffn2_w_gate_up), bf(ffn2_w_down)
    win, wout, wple, wgate = bf(w_in), bf(w_out), bf(w_ple), bf(w_ple_gate)
    g1, gm, g2, gple = rows(ffn1_norm), rows(mix_norm), rows(ffn2_norm), rows(ple_norm)
    cb, lng, lnb = rows(conv_b), rows(conv_ln_g), rows(conv_ln_b)
    an, cn = rows(attn_out_norm), rows(conv_out_norm)
    gfin = final_norm.reshape(1, 1, d)
    p_rows = p.reshape(depth, n, p.shape[3])

    h = x.reshape(n, d)
    for i in range(depth):
        h = _ffn(h, i, g1, wgu1, wd1)
        q, kx, vx, u = _in_proj(h, i, gm, win, cos, sin)
        h = _mix_ffn(h, i, q, kx, vx, u, sinks, conv_w, cb, lng, lnb, an, cn, wout,
                     g2, wgu2, wd2, p_rows, wple, gple, wgate, gfin,
                     seq_len=s, final=(i == depth - 1))
    return h.reshape(b, s, d)
```

```python
import functools

import jax
import jax.numpy as jnp
from jax import lax
from jax.experimental import pallas as pl
from jax.experimental.pallas import tpu as pltpu

F32 = jnp.float32
BF16 = jnp.bfloat16

HEAD_DIM = 64
N_HEADS = 8
N_KV_HEADS = 2
GROUP = N_HEADS // N_KV_HEADS
ATTN_W = N_HEADS * HEAD_DIM
KV_W = N_KV_HEADS * HEAD_DIM
CONV_WIDTH = 31
WINDOW = 128
ROT_DIM = HEAD_DIM // 4
ROPE_THETA = 500000.0
EPS = 1e-6
MASK_VALUE = -1e30

LANES = 128
SUBLANES = 8

FFN_ROWS = 1024
FFN_CHUNK = 256
PROJ_ROWS = 1024
MIX_ROWS = 512
CONV_SUB = 64
CONV_HALO = 32
TABLE_ROWS = 2048
VMEM_LIMIT = 56 * 1024 * 1024


def _rms(x, g):
    return x * lax.rsqrt(jnp.mean(x * x, axis=-1, keepdims=True) + EPS) * g


def _dot(a, b):
    return jnp.dot(a, b, preferred_element_type=F32)


def _dot_nt(a, b):
    return lax.dot_general(a, b, (((1,), (1,)), ((), ())), preferred_element_type=F32)


def _layer_spec(arr, layer):
    tail = arr.shape[1:]
    idx = (layer,) + (0,) * len(tail)
    return pl.BlockSpec((None,) + tail, lambda *_: idx, pipeline_mode=pl.Buffered(1))


def _params(n_axes):
    return pltpu.CompilerParams(
        dimension_semantics=("arbitrary",) * n_axes, vmem_limit_bytes=VMEM_LIMIT)


ROPE_GROUPS = LANES // ROT_DIM


def _rope_table_body(pos_ref, invf_ref, cos_ref, sin_ref):
    ang = pos_ref[...] * invf_ref[...]
    cos_ref[...] = jnp.cos(ang)
    sin_ref[...] = jnp.sin(ang)


def _rope_tables(pos, invf_lane):
    n = pos.shape[0]
    out = jax.ShapeDtypeStruct((n, LANES), F32)
    return pl.pallas_call(
        _rope_table_body,
        out_shape=(out, out),
        grid=(n // TABLE_ROWS,),
        in_specs=[pl.BlockSpec((TABLE_ROWS, LANES), lambda i: (i, 0)),
                  pl.BlockSpec((1, LANES), lambda i: (0, 0))],
        out_specs=(pl.BlockSpec((TABLE_ROWS, LANES), lambda i: (i, 0)),
                   pl.BlockSpec((TABLE_ROWS, LANES), lambda i: (i, 0))),
        compiler_params=_params(1),
        name="rope_table",
    )(pos, invf_lane)


def _swiglu_chunk(xn_ref, wgu_ref, wd_ref, c):
    ff = wd_ref.shape[0]
    lo = c * FFN_CHUNK
    xn = xn_ref[...]
    g = _dot(xn, wgu_ref[:, lo:lo + FFN_CHUNK])
    u = _dot(xn, wgu_ref[:, ff + lo:ff + lo + FFN_CHUNK])
    a = (g * jax.nn.sigmoid(g) * u).astype(BF16)
    return _dot(a, wd_ref[lo:lo + FFN_CHUNK, :])


ANCHOR_ROWS = 16


def _anchor(ref, row0, value, always):
    rows = slice(row0, row0 + ANCHOR_ROWS)
    ref[rows, :LANES] = jnp.where(always, ref[rows, :LANES], value[:ANCHOR_ROWS, :LANES].astype(ref.dtype))


def _swiglu(xn_ref, wgu_ref, wd_ref, acc_ref, units=None, always=None):
    n_chunks = wd_ref.shape[0] // FFN_CHUNK
    y = None
    d = None
    for c in range(n_chunks):
        if always is not None and d is not None:
            _anchor(xn_ref, 0, d, always)
        d = _swiglu_chunk(xn_ref, wgu_ref, wd_ref, c)
        if c == 0:
            acc_ref[...] = d
        elif c < n_chunks - 1:
            acc_ref[...] += d
        else:
            y = acc_ref[...] + d
        for thunk in (units[c] if units else ()):
            thunk()
    return y


def _ffn_body(x_ref, g_ref, wgu_ref, wd_ref, o_ref, xn_ref, acc_ref):
    xn_ref[...] = _rms(x_ref[...], g_ref[...]).astype(BF16)
    y = _swiglu(xn_ref, wgu_ref, wd_ref, acc_ref)
    o_ref[...] = x_ref[...] + 0.5 * y


def _ffn(h, layer, g, wgu, wd):
    n, d = h.shape
    row = pl.BlockSpec((FFN_ROWS, d), lambda i: (i, 0))
    return pl.pallas_call(
        _ffn_body,
        out_shape=jax.ShapeDtypeStruct((n, d), F32),
        grid=(n // FFN_ROWS,),
        in_specs=[row, _layer_spec(g, layer), _layer_spec(wgu, layer), _layer_spec(wd, layer)],
        out_specs=row,
        scratch_shapes=[pltpu.VMEM((FFN_ROWS, d), BF16), pltpu.VMEM((FFN_ROWS, d), F32)],
        compiler_params=_params(1),
        name="ffn",
    )(h, g, wgu, wd)


def _in_proj_body(h_ref, g_ref, win_ref, cos_ref, sin_ref, q_ref, kx_ref, vx_ref, u_ref, *, tiles_per_group):
    xn = _rms(h_ref[...], g_ref[...]).astype(BF16)
    lane = lax.broadcasted_iota(jnp.int32, (1, LANES), 1)
    in_head = lane % HEAD_DIM
    half = ROT_DIM // 2
    group = pl.program_id(0) // tiles_per_group
    src = jnp.broadcast_to(group * ROT_DIM + jnp.where(in_head < ROT_DIM, in_head, 0), cos_ref.shape)
    cos = jnp.where(in_head < ROT_DIM, jnp.take_along_axis(cos_ref[...], src, axis=1), 1.0)
    sin = jnp.take_along_axis(sin_ref[...], src, axis=1)
    s_lo = jnp.where(in_head < half, -sin, 0.0)
    s_hi = jnp.where((in_head >= half) & (in_head < ROT_DIM), sin, 0.0)

    def rope(t):
        return (t * cos + pltpu.roll(t, LANES - half, 1) * s_lo + pltpu.roll(t, half, 1) * s_hi)

    k_col = ATTN_W
    v_col = k_col + KV_W
    a_col = v_col + KV_W
    conv_ch = (win_ref.shape[1] - a_col) // 2

    scale = HEAD_DIM ** -0.5
    q = _dot(xn, win_ref[:, :k_col])
    for p in range(ATTN_W // LANES):
        cols = slice(p * LANES, (p + 1) * LANES)
        q_ref[:, cols] = (rope(q[:, cols]) * scale).astype(BF16)

    kv = _dot(xn, win_ref[:, k_col:a_col])
    low = lane < HEAD_DIM

    def head_pair_layout(t, out_ref):
        sw = pltpu.roll(t, HEAD_DIM, 1)
        out_ref[:, 0 * LANES:1 * LANES] = jnp.where(low, t, 0.0).astype(BF16)
        out_ref[:, 1 * LANES:2 * LANES] = jnp.where(low, 0.0, sw).astype(BF16)
        out_ref[:, 2 * LANES:3 * LANES] = jnp.where(low, sw, 0.0).astype(BF16)
        out_ref[:, 3 * LANES:4 * LANES] = jnp.where(low, 0.0, t).astype(BF16)

    head_pair_layout(rope(kv[:, :KV_W]), kx_ref)
    head_pair_layout(kv[:, KV_W:], vx_ref)

    ag = _dot(xn, win_ref[:, a_col:])
    u_ref[...] = ag[:, :conv_ch] * jax.nn.sigmoid(ag[:, conv_ch:])


def _in_proj(h, layer, g, win, cos, sin):
    n, d = h.shape
    rows = lambda w: pl.BlockSpec((PROJ_ROWS, w), lambda i: (i, 0))
    conv_ch = (win.shape[2] - ATTN_W - 2 * KV_W) // 2
    tiles_per_group = cos.shape[0] // PROJ_ROWS
    table = pl.BlockSpec((PROJ_ROWS, LANES), lambda i: (i % tiles_per_group, 0))
    return pl.pallas_call(
        functools.partial(_in_proj_body, tiles_per_group=tiles_per_group),
        out_shape=(jax.ShapeDtypeStruct((n, ATTN_W), BF16),
                   jax.ShapeDtypeStruct((n, 4 * LANES), BF16),
                   jax.ShapeDtypeStruct((n, 4 * LANES), BF16),
                   jax.ShapeDtypeStruct((n, conv_ch), F32)),
        grid=(n // PROJ_ROWS,),
        in_specs=[rows(d), _layer_spec(g, layer), _layer_spec(win, layer), table, table],
        out_specs=(rows(ATTN_W), rows(4 * LANES), rows(4 * LANES), rows(conv_ch)),
        compiler_params=_params(1),
        name="in_proj",
    )(h, g, win, cos, sin)


def _conv_rows(uall_ref, cw_ref, base):
    first_tap = CONV_HALO - (CONV_WIDTH - 1)
    tiles = []
    for j in range(uall_ref.shape[0]):
        lanes = slice(j * LANES, (j + 1) * LANES)
        acc = None
        for k in range(CONV_WIDTH):
            start = base + first_tap + k
            term = cw_ref[k:k + 1, lanes] * uall_ref[j, start:start + CONV_SUB, :]
            acc = term if acc is None else acc + term
        tiles.append(acc)
    return jnp.concatenate(tiles, axis=1)


def _mix_ffn_body(sinks_ref, h_ref, q_ref, kc_ref, kp_ref, vc_ref, vp_ref, uc_ref, up_ref,
                  cw_ref, cb_ref, lng_ref, lnb_ref, an_ref, cn_ref, wout_ref,
                  g2_ref, wgu_ref, wd_ref, p_ref, wple_ref, gple_ref, wgate_ref, gfin_ref,
                  o_ref, kall_ref, vall_ref, uall_ref, mixed_ref, hmid_ref, xn_ref, acc_ref,
                  *, layer, tiles_per_seq, n_tiles, final):
    t = pl.program_id(0)
    ts = h_ref.shape[0]
    not_first = jnp.minimum(t, n_tiles - 1) % tiles_per_seq > 0

    @pl.when(t == 0)
    def _():
        hmid_ref[...] = jnp.zeros_like(hmid_ref)

    h_prev = hmid_ref[...]
    o_ref[...] = h_prev
    xn_ref[...] = _rms(h_prev, g2_ref[...]).astype(BF16)

    kall_ref[:WINDOW] = kp_ref[...]
    kall_ref[WINDOW:] = kc_ref[...]
    vall_ref[:WINDOW] = vp_ref[...]
    vall_ref[WINDOW:] = vc_ref[...]
    for j in range(uall_ref.shape[0]):
        lanes = slice(j * LANES, (j + 1) * LANES)
        uall_ref[j, :CONV_HALO, :] = jnp.where(not_first, up_ref[:, lanes], 0.0)
        uall_ref[j, CONV_HALO:, :] = uc_ref[:, lanes]

    r = lax.broadcasted_iota(jnp.int32, (WINDOW, WINDOW), 0)
    c = lax.broadcasted_iota(jnp.int32, (WINDOW, WINDOW), 1)
    from_prev = c > r
    no_prev = from_prev & jnp.logical_not(not_first)

    def kv_tiles(p):
        grp = (2 * p) // GROUP
        return (slice((2 * grp) * LANES, (2 * grp + 1) * LANES),
                slice((2 * grp + 1) * LANES, (2 * grp + 2) * LANES))

    def softmax_sink(s, first_block, sink):
        s = jnp.where(from_prev, s[:, :WINDOW], s[:, WINDOW:])
        if first_block:
            s = jnp.where(no_prev, MASK_VALUE, s)
        m = jnp.maximum(jnp.max(s, axis=-1, keepdims=True), sink)
        e = jnp.exp(s - m)
        den = jnp.sum(e, axis=-1, keepdims=True) + jnp.exp(sink - m)
        p = e * (1.0 / den)
        return jnp.concatenate([jnp.where(from_prev, p, 0.0), jnp.where(from_prev, 0.0, p)],
                               axis=1).astype(BF16)

    always = t >= 0
    chain = {}

    def chain_after(ref, row0):
        if "token" in chain:
            _anchor(ref, row0, chain["token"], always)

    def conv_module(sb):
        base = sb * CONV_SUB
        chain_after(uall_ref.at[0], base)
        y = _conv_rows(uall_ref, cw_ref, base) + cb_ref[...]
        chain["token"] = y
        mu = jnp.mean(y, axis=-1, keepdims=True)
        yc = y - mu
        yn = yc * lax.rsqrt(jnp.mean(yc * yc, axis=-1, keepdims=True) + EPS) * lng_ref[...] + lnb_ref[...]
        z = yn * jax.nn.sigmoid(yn)
        out = _rms(z, cn_ref[...]).astype(BF16)
        mixed_ref[base:base + CONV_SUB, ATTN_W:] = out

    conv_per_block = WINDOW // CONV_SUB
    stages = []
    for qb in range(ts // WINDOW):
        rows = slice(qb * WINDOW, (qb + 1) * WINDOW)
        keys = slice(qb * WINDOW, (qb + 2) * WINDOW)
        first_block = qb == 0
        state = {}

        def scores_stage(rows=rows, keys=keys, state=state):
            chain_after(kall_ref, keys.start)
            scores = []
            for p in range(N_HEADS // 2):
                lo, hi = kv_tiles(p)
                qp = q_ref[rows, p * LANES:(p + 1) * LANES]
                scores.append(_dot_nt(qp, kall_ref[keys, lo]))
                scores.append(_dot_nt(qp, kall_ref[keys, hi]))
            state["scores"] = scores
            chain["token"] = scores[0]

        def softmax_stage(first_block=first_block, state=state):
            state["probs"] = [softmax_sink(s, first_block, sinks_ref[layer, hd])
                              for hd, s in enumerate(state.pop("scores"))]
            chain["token"] = state["probs"][0]

        def values_stage(rows=rows, keys=keys, state=state):
            chain_after(vall_ref, keys.start)
            probs = state.pop("probs")
            outs = []
            for p in range(N_HEADS // 2):
                lo, hi = kv_tiles(p)
                outs.append(_dot(probs[2 * p], vall_ref[keys, lo])
                            + _dot(probs[2 * p + 1], vall_ref[keys, hi]))
            mixed_ref[rows, :ATTN_W] = _rms(jnp.concatenate(outs, axis=1), an_ref[...]).astype(BF16)
            chain["token"] = outs[0]

        conv_stages = [functools.partial(conv_module, qb * conv_per_block + sb)
                       for sb in range(conv_per_block)]
        stages += [scores_stage, conv_stages[0], softmax_stage, *conv_stages[1:], values_stage]

    n_chunks = wd_ref.shape[0] // FFN_CHUNK
    per_chunk = -(-len(stages) // n_chunks)
    units = [stages[c * per_chunk:(c + 1) * per_chunk] for c in range(n_chunks)]
    y = _swiglu(xn_ref, wgu_ref, wd_ref, acc_ref, units, always)

    hmid_ref[...] = h_ref[...] + _dot(mixed_ref[...], wout_ref[...])

    h = o_ref[...] + 0.5 * y
    e = _rms(_dot(p_ref[...].astype(BF16), wple_ref[...]), gple_ref[...])
    h = h + jax.nn.sigmoid(_dot(h.astype(BF16), wgate_ref[...])) * e
    if final:
        h = _rms(h, gfin_ref[...])
    o_ref[...] = h


def _mix_ffn(h, layer, q, kx, vx, u, sinks, cw, cb, lng, lnb, an, cn, wout,
             g2, wgu, wd, p, wple, gple, wgate, gfin, *, seq_len, final):
    n, d = h.shape
    conv_ch = u.shape[1]
    ts = MIX_ROWS
    n_tiles = n // ts
    mix_tile = lambda t: jnp.minimum(t, n_tiles - 1)
    ffn_tile = lambda t: jnp.maximum(t - 1, 0)
    cur = lambda w: pl.BlockSpec((ts, w), lambda t: (mix_tile(t), 0))
    prev = lambda rows, w: pl.BlockSpec(
        (rows, w), lambda t: (jnp.maximum(mix_tile(t) * (ts // rows) - 1, 0), 0))
    return pl.pallas_call(
        functools.partial(_mix_ffn_body, layer=layer, tiles_per_seq=seq_len // ts, n_tiles=n_tiles,
                          final=final),
        out_shape=jax.ShapeDtypeStruct((n, d), F32),
        grid=(n_tiles + 1,),
        in_specs=[pl.BlockSpec(memory_space=pltpu.SMEM),
                  cur(d), cur(ATTN_W),
                  cur(4 * LANES), prev(WINDOW, 4 * LANES),
                  cur(4 * LANES), prev(WINDOW, 4 * LANES),
                  cur(conv_ch), prev(CONV_HALO, conv_ch),
                  _layer_spec(cw, layer), _layer_spec(cb, layer), _layer_spec(lng, layer),
                  _layer_spec(lnb, layer), _layer_spec(an, layer), _layer_spec(cn, layer),
                  _layer_spec(wout, layer),
                  _layer_spec(g2, layer), _layer_spec(wgu, layer), _layer_spec(wd, layer),
                  pl.BlockSpec((None, ts, p.shape[2]), lambda t: (layer, ffn_tile(t), 0)),
                  _layer_spec(wple, layer), _layer_spec(gple, layer), _layer_spec(wgate, layer),
                  _layer_spec(gfin, 0)],
        out_specs=pl.BlockSpec((ts, d), lambda t: (ffn_tile(t), 0)),
        scratch_shapes=[pltpu.VMEM((ts + WINDOW, 4 * LANES), BF16),
                        pltpu.VMEM((ts + WINDOW, 4 * LANES), BF16),
                        pltpu.VMEM((conv_ch // LANES, ts + CONV_HALO, LANES), F32),
                        pltpu.VMEM((ts, d), BF16),
                        pltpu.VMEM((ts, d), F32),
                        pltpu.VMEM((ts, d), BF16),
                        pltpu.VMEM((ts, d), F32)],
        compiler_params=_params(1),
        name="mix_ffn",
    )(sinks, h, q, kx, kx, vx, vx, u, u, cw, cb, lng, lnb, an, cn, wout,
      g2, wgu, wd, p, wple, gple, wgate, gfin)


def kernel(x, p, positions, ffn1_norm, ffn1_w_gate_up, ffn1_w_down, mix_norm, w_in, sinks, conv_w, conv_b, conv_ln_g, conv_ln_b, attn_out_norm, conv_out_norm, w_out, ffn2_norm, ffn2_w_gate_up, ffn2_w_down, w_ple, ple_norm, w_ple_gate, final_norm):
    b, s, d = x.shape
    depth = p.shape[0]
    n = b * s
    rows = lambda v: v.reshape(v.shape[0], 1, v.shape[1])
    bf = lambda w: w.astype(BF16)

    inv_freq = ROPE_THETA ** (-jnp.arange(0, ROT_DIM, 2, dtype=F32) / ROT_DIM)
    invf_lane = inv_freq[jnp.arange(LANES) % (ROT_DIM // 2)].reshape(1, LANES)
    group_rows = n // ROPE_GROUPS
    assert n % ROPE_GROUPS == 0 and group_rows % PROJ_ROWS == 0 and group_rows % TABLE_ROWS == 0
    pos_groups = positions.astype(F32).reshape(ROPE_GROUPS, group_rows).T
    cos, sin = _rope_tables(jnp.repeat(pos_groups, ROT_DIM, axis=1), invf_lane)

    wgu1, wd1, wgu2, wd2 = bf(ffn1_w_gate_up), bf(ffn1_w_down), bf(ffn2_w_gate_up), bf(ffn2_w_down)
    win, wout, wple, wgate = bf(w_in), bf(w_out), bf(w_ple), bf(w_ple_gate)
    g1, gm, g2, gple = rows(ffn1_norm), rows(mix_norm), rows(ffn2_norm), rows(ple_norm)
    cb, lng, lnb = rows(conv_b), rows(conv_ln_g), rows(conv_ln_b)
    an, cn = rows(attn_out_norm), rows(conv_out_norm)
    gfin = final_norm.reshape(1, 1, d)
    p_rows = p.reshape(depth, n, p.shape[3])

    h = x.reshape(n, d)
    for i in range(depth):
        h = _ffn(h, i, g1, wgu1, wd1)
        q, kx, vx, u = _in_proj(h, i, gm, win, cos, sin)
        h = _mix_ffn(h, i, q, kx, vx, u, sinks, conv_w, cb, lng, lnb, an, cn, wout,
                     g2, wgu2, wd2, p_rows, wple, gple, wgate, gfin,
                     seq_len=s, final=(i == depth - 1))
    return h.reshape(b, s, d)
```

```python
import functools

import jax
import jax.numpy as jnp
from jax import lax
from jax.experimental import pallas as pl
from jax.experimental.pallas import tpu as pltpu

F32 = jnp.float32
BF16 = jnp.bfloat16

HEAD_DIM = 64
N_HEADS = 8
N_KV_HEADS = 2
GROUP = N_HEADS // N_KV_HEADS
ATTN_W = N_HEADS * HEAD_DIM
KV_W = N_KV_HEADS * HEAD_DIM
CONV_WIDTH = 31
WINDOW = 128
ROT_DIM = HEAD_DIM // 4
ROPE_THETA = 500000.0
EPS = 1e-6
MASK_VALUE = -1e30

LANES = 128
SUBLANES = 8

FFN_ROWS = 1024
FFN_CHUNK = 256
PROJ_ROWS = 1024
MIX_ROWS = 512
CONV_SUB = 16
CONV_HALO = 32
TABLE_ROWS = 2048
VMEM_LIMIT = 56 * 1024 * 1024


def _rms(x, g):
    return x * lax.rsqrt(jnp.mean(x * x, axis=-1, keepdims=True) + EPS) * g


def _dot(a, b):
    return jnp.dot(a, b, preferred_element_type=F32)


def _dot_nt(a, b):
    return lax.dot_general(a, b, (((1,), (1,)), ((), ())), preferred_element_type=F32)


def _layer_spec(arr, layer):
    tail = arr.shape[1:]
    idx = (layer,) + (0,) * len(tail)
    return pl.BlockSpec((None,) + tail, lambda *_: idx, pipeline_mode=pl.Buffered(1))


def _params(n_axes):
    return pltpu.CompilerParams(
        dimension_semantics=("arbitrary",) * n_axes, vmem_limit_bytes=VMEM_LIMIT)


ROPE_GROUPS = LANES // ROT_DIM


def _rope_table_body(pos_ref, invf_ref, cos_ref, sin_ref):
    ang = pos_ref[...] * invf_ref[...]
    cos_ref[...] = jnp.cos(ang)
    sin_ref[...] = jnp.sin(ang)


def _rope_tables(pos, invf_lane):
    n = pos.shape[0]
    out = jax.ShapeDtypeStruct((n, LANES), F32)
    return pl.pallas_call(
        _rope_table_body,
        out_shape=(out, out),
        grid=(n // TABLE_ROWS,),
        in_specs=[pl.BlockSpec((TABLE_ROWS, LANES), lambda i: (i, 0)),
                  pl.BlockSpec((1, LANES), lambda i: (0, 0))],
        out_specs=(pl.BlockSpec((TABLE_ROWS, LANES), lambda i: (i, 0)),
                   pl.BlockSpec((TABLE_ROWS, LANES), lambda i: (i, 0))),
        compiler_params=_params(1),
        name="rope_table",
    )(pos, invf_lane)


def _swiglu_chunk(xn_ref, wgu_ref, wd_ref, c):
    ff = wd_ref.shape[0]
    lo = c * FFN_CHUNK
    xn = xn_ref[...]
    g = _dot(xn, wgu_ref[:, lo:lo + FFN_CHUNK])
    u = _dot(xn, wgu_ref[:, ff + lo:ff + lo + FFN_CHUNK])
    a = (g * jax.nn.sigmoid(g) * u).astype(BF16)
    return _dot(a, wd_ref[lo:lo + FFN_CHUNK, :])


ANCHOR_ROWS = 16


def _anchor(ref, row0, value, always):
    rows = slice(row0, row0 + ANCHOR_ROWS)
    ref[rows, :LANES] = jnp.where(always, ref[rows, :LANES], value[:ANCHOR_ROWS, :LANES].astype(ref.dtype))


def _swiglu(xn_ref, wgu_ref, wd_ref, acc_ref, units=None, always=None):
    n_chunks = wd_ref.shape[0] // FFN_CHUNK
    y = None
    d = None
    for c in range(n_chunks):
        if always is not None and d is not None:
            _anchor(xn_ref, 0, d, always)
        d = _swiglu_chunk(xn_ref, wgu_ref, wd_ref, c)
        if c == 0:
            acc_ref[...] = d
        elif c < n_chunks - 1:
            acc_ref[...] += d
        else:
            y = acc_ref[...] + d
        for thunk in (units[c] if units else ()):
            thunk()
    return y


def _ffn_body(x_ref, g_ref, wgu_ref, wd_ref, o_ref, xn_ref, acc_ref):
    xn_ref[...] = _rms(x_ref[...], g_ref[...]).astype(BF16)
    y = _swiglu(xn_ref, wgu_ref, wd_ref, acc_ref)
    o_ref[...] = x_ref[...] + 0.5 * y


def _ffn(h, layer, g, wgu, wd):
    n, d = h.shape
    row = pl.BlockSpec((FFN_ROWS, d), lambda i: (i, 0))
    return pl.pallas_call(
        _ffn_body,
        out_shape=jax.ShapeDtypeStruct((n, d), F32),
        grid=(n // FFN_ROWS,),
        in_specs=[row, _layer_spec(g, layer), _layer_spec(wgu, layer), _layer_spec(wd, layer)],
        out_specs=row,
        scratch_shapes=[pltpu.VMEM((FFN_ROWS, d), BF16), pltpu.VMEM((FFN_ROWS, d), F32)],
        compiler_params=_params(1),
        name="ffn",
    )(h, g, wgu, wd)


def _in_proj_body(h_ref, g_ref, win_ref, cos_ref, sin_ref, q_ref, kx_ref, vx_ref, u_ref, *, tiles_per_group):
    xn = _rms(h_ref[...], g_ref[...]).astype(BF16)
    lane = lax.broadcasted_iota(jnp.int32, (1, LANES), 1)
    in_head = lane % HEAD_DIM
    half = ROT_DIM // 2
    group = pl.program_id(0) // tiles_per_group
    src = jnp.broadcast_to(group * ROT_DIM + jnp.where(in_head < ROT_DIM, in_head, 0), cos_ref.shape)
    cos = jnp.where(in_head < ROT_DIM, jnp.take_along_axis(cos_ref[...], src, axis=1), 1.0)
    sin = jnp.take_along_axis(sin_ref[...], src, axis=1)
    s_lo = jnp.where(in_head < half, -sin, 0.0)
    s_hi = jnp.where((in_head >= half) & (in_head < ROT_DIM), sin, 0.0)

    def rope(t):
        return (t * cos + pltpu.roll(t, LANES - half, 1) * s_lo + pltpu.roll(t, half, 1) * s_hi)

    k_col = ATTN_W
    v_col = k_col + KV_W
    a_col = v_col + KV_W
    conv_ch = (win_ref.shape[1] - a_col) // 2

    scale = HEAD_DIM ** -0.5
    q = _dot(xn, win_ref[:, :k_col])
    for p in range(ATTN_W // LANES):
        cols = slice(p * LANES, (p + 1) * LANES)
        q_ref[:, cols] = (rope(q[:, cols]) * scale).astype(BF16)

    kv = _dot(xn, win_ref[:, k_col:a_col])
    low = lane < HEAD_DIM

    def head_pair_layout(t, out_ref):
        sw = pltpu.roll(t, HEAD_DIM, 1)
        out_ref[:, 0 * LANES:1 * LANES] = jnp.where(low, t, 0.0).astype(BF16)
        out_ref[:, 1 * LANES:2 * LANES] = jnp.where(low, 0.0, sw).astype(BF16)
        out_ref[:, 2 * LANES:3 * LANES] = jnp.where(low, sw, 0.0).astype(BF16)
        out_ref[:, 3 * LANES:4 * LANES] = jnp.where(low, 0.0, t).astype(BF16)

    head_pair_layout(rope(kv[:, :KV_W]), kx_ref)
    head_pair_layout(kv[:, KV_W:], vx_ref)

    ag = _dot(xn, win_ref[:, a_col:])
    u_ref[...] = ag[:, :conv_ch] * jax.nn.sigmoid(ag[:, conv_ch:])


def _in_proj(h, layer, g, win, cos, sin):
    n, d = h.shape
    rows = lambda w: pl.BlockSpec((PROJ_ROWS, w), lambda i: (i, 0))
    conv_ch = (win.shape[2] - ATTN_W - 2 * KV_W) // 2
    tiles_per_group = cos.shape[0] // PROJ_ROWS
    table = pl.BlockSpec((PROJ_ROWS, LANES), lambda i: (i % tiles_per_group, 0))
    return pl.pallas_call(
        functools.partial(_in_proj_body, tiles_per_group=tiles_per_group),
        out_shape=(jax.ShapeDtypeStruct((n, ATTN_W), BF16),
                   jax.ShapeDtypeStruct((n, 4 * LANES), BF16),
                   jax.ShapeDtypeStruct((n, 4 * LANES), BF16),
                   jax.ShapeDtypeStruct((n, conv_ch), F32)),
        grid=(n // PROJ_ROWS,),
        in_specs=[rows(d), _layer_spec(g, layer), _layer_spec(win, layer), table, table],
        out_specs=(rows(ATTN_W), rows(4 * LANES), rows(4 * LANES), rows(conv_ch)),
        compiler_params=_params(1),
        name="in_proj",
    )(h, g, win, cos, sin)


def _conv_rows(uall_ref, cw_ref, base):
    first_tap = CONV_HALO - (CONV_WIDTH - 1)
    tiles = []
    for j in range(uall_ref.shape[0]):
        lanes = slice(j * LANES, (j + 1) * LANES)
        acc = None
        for k in range(CONV_WIDTH):
            start = base + first_tap + k
            term = cw_ref[k:k + 1, lanes] * uall_ref[j, start:start + CONV_SUB, :]
            acc = term if acc is None else acc + term
        tiles.append(acc)
    return jnp.concatenate(tiles, axis=1)


def _mix_ffn_body(sinks_ref, h_ref, q_ref, kc_ref, kp_ref, vc_ref, vp_ref, uc_ref, up_ref,
                  cw_ref, cb_ref, lng_ref, lnb_ref, an_ref, cn_ref, wout_ref,
                  g2_ref, wgu_ref, wd_ref, p_ref, wple_ref, gple_ref, wgate_ref, gfin_ref,
                  o_ref, kall_ref, vall_ref, uall_ref, mixed_ref, hmid_ref, xn_ref, acc_ref,
                  *, layer, tiles_per_seq, n_tiles, final):
    t = pl.program_id(0)
    ts = h_ref.shape[0]
    not_first = jnp.minimum(t, n_tiles - 1) % tiles_per_seq > 0

    @pl.when(t == 0)
    def _():
        hmid_ref[...] = jnp.zeros_like(hmid_ref)

    h_prev = hmid_ref[...]
    o_ref[...] = h_prev
    xn_ref[...] = _rms(h_prev, g2_ref[...]).astype(BF16)

    kall_ref[:WINDOW] = kp_ref[...]
    kall_ref[WINDOW:] = kc_ref[...]
    vall_ref[:WINDOW] = vp_ref[...]
    vall_ref[WINDOW:] = vc_ref[...]
    for j in range(uall_ref.shape[0]):
        lanes = slice(j * LANES, (j + 1) * LANES)
        uall_ref[j, :CONV_HALO, :] = jnp.where(not_first, up_ref[:, lanes], 0.0)
        uall_ref[j, CONV_HALO:, :] = uc_ref[:, lanes]

    r = lax.broadcasted_iota(jnp.int32, (WINDOW, WINDOW), 0)
    c = lax.broadcasted_iota(jnp.int32, (WINDOW, WINDOW), 1)
    from_prev = c > r
    no_prev = from_prev & jnp.logical_not(not_first)

    def kv_tiles(p):
        grp = (2 * p) // GROUP
        return (slice((2 * grp) * LANES, (2 * grp + 1) * LANES),
                slice((2 * grp + 1) * LANES, (2 * grp + 2) * LANES))

    def softmax_sink(s, first_block, sink):
        s = jnp.where(from_prev, s[:, :WINDOW], s[:, WINDOW:])
        if first_block:
            s = jnp.where(no_prev, MASK_VALUE, s)
        m = jnp.maximum(jnp.max(s, axis=-1, keepdims=True), sink)
        e = jnp.exp(s - m)
        den = jnp.sum(e, axis=-1, keepdims=True) + jnp.exp(sink - m)
        p = e * (1.0 / den)
        return jnp.concatenate([jnp.where(from_prev, p, 0.0), jnp.where(from_prev, 0.0, p)],
                               axis=1).astype(BF16)

    always = t >= 0
    chain = {}

    def chain_after(ref, row0):
        if "token" in chain:
            _anchor(ref, row0, chain["token"], always)

    def conv_module(sb):
        base = sb * CONV_SUB
        chain_after(uall_ref.at[0], base)
        y = _conv_rows(uall_ref, cw_ref, base) + cb_ref[...]
        chain["token"] = y
        mu = jnp.mean(y, axis=-1, keepdims=True)
        yc = y - mu
        yn = yc * lax.rsqrt(jnp.mean(yc * yc, axis=-1, keepdims=True) + EPS) * lng_ref[...] + lnb_ref[...]
        z = yn * jax.nn.sigmoid(yn)
        out = _rms(z, cn_ref[...]).astype(BF16)
        mixed_ref[base:base + CONV_SUB, ATTN_W:] = out

    conv_per_block = WINDOW // CONV_SUB
    stages = []
    for qb in range(ts // WINDOW):
        rows = slice(qb * WINDOW, (qb + 1) * WINDOW)
        keys = slice(qb * WINDOW, (qb + 2) * WINDOW)
        first_block = qb == 0
        state = {}

        def scores_stage(rows=rows, keys=keys, state=state):
            chain_after(kall_ref, keys.start)
            scores = []
            for p in range(N_HEADS // 2):
                lo, hi = kv_tiles(p)
                qp = q_ref[rows, p * LANES:(p + 1) * LANES]
                scores.append(_dot_nt(qp, kall_ref[keys, lo]))
                scores.append(_dot_nt(qp, kall_ref[keys, hi]))
            state["scores"] = scores
            chain["token"] = scores[0]

        def softmax_stage(first_block=first_block, state=state):
            state["probs"] = [softmax_sink(s, first_block, sinks_ref[layer, hd])
                              for hd, s in enumerate(state.pop("scores"))]
            chain["token"] = state["probs"][0]

        def values_stage(rows=rows, keys=keys, state=state):
            chain_after(vall_ref, keys.start)
            probs = state.pop("probs")
            outs = []
            for p in range(N_HEADS // 2):
                lo, hi = kv_tiles(p)
                outs.append(_dot(probs[2 * p], vall_ref[keys, lo])
                            + _dot(probs[2 * p + 1], vall_ref[keys, hi]))
            mixed_ref[rows, :ATTN_W] = _rms(jnp.concatenate(outs, axis=1), an_ref[...]).astype(BF16)
            chain["token"] = outs[0]

        attn_stages = [scores_stage, softmax_stage, values_stage]
        conv_stages = [functools.partial(conv_module, qb * conv_per_block + sb)
                       for sb in range(conv_per_block)]
        done = 0
        for k, stage in enumerate(attn_stages):
            stages.append(stage)
            due = (k + 1) * len(conv_stages) // len(attn_stages)
            stages += conv_stages[done:due]
            done = due

    n_chunks = wd_ref.shape[0] // FFN_CHUNK
    per_chunk = -(-len(stages) // n_chunks)
    units = [stages[c * per_chunk:(c + 1) * per_chunk] for c in range(n_chunks)]
    y = _swiglu(xn_ref, wgu_ref, wd_ref, acc_ref, units, always)

    hmid_ref[...] = h_ref[...] + _dot(mixed_ref[...], wout_ref[...])

    h = o_ref[...] + 0.5 * y
    e = _rms(_dot(p_ref[...].astype(BF16), wple_ref[...]), gple_ref[...])
    h = h + jax.nn.sigmoid(_dot(h.astype(BF16), wgate_ref[...])) * e
    if final:
        h = _rms(h, gfin_ref[...])
    o_ref[...] = h


def _mix_ffn(h, layer, q, kx, vx, u, sinks, cw, cb, lng, lnb, an, cn, wout,
             g2, wgu, wd, p, wple, gple, wgate, gfin, *, seq_len, final):
    n, d = h.shape
    conv_ch = u.shape[1]
    ts = MIX_ROWS
    n_tiles = n // ts
    mix_tile = lambda t: jnp.minimum(t, n_tiles - 1)
    ffn_tile = lambda t: jnp.maximum(t - 1, 0)
    cur = lambda w: pl.BlockSpec((ts, w), lambda t: (mix_tile(t), 0))
    prev = lambda rows, w: pl.BlockSpec(
        (rows, w), lambda t: (jnp.maximum(mix_tile(t) * (ts // rows) - 1, 0), 0))
    return pl.pallas_call(
        functools.partial(_mix_ffn_body, layer=layer, tiles_per_seq=seq_len // ts, n_tiles=n_tiles,
                          final=final),
        out_shape=jax.ShapeDtypeStruct((n, d), F32),
        grid=(n_tiles + 1,),
        in_specs=[pl.BlockSpec(memory_space=pltpu.SMEM),
                  cur(d), cur(ATTN_W),
                  cur(4 * LANES), prev(WINDOW, 4 * LANES),
                  cur(4 * LANES), prev(WINDOW, 4 * LANES),
                  cur(conv_ch), prev(CONV_HALO, conv_ch),
                  _layer_spec(cw, layer), _layer_spec(cb, layer), _layer_spec(lng, layer),
                  _layer_spec(lnb, layer), _layer_spec(an, layer), _layer_spec(cn, layer),
                  _layer_spec(wout, layer),
                  _layer_spec(g2, layer), _layer_spec(wgu, layer), _layer_spec(wd, layer),
                  pl.BlockSpec((None, ts, p.shape[2]), lambda t: (layer, ffn_tile(t), 0)),
                  _layer_spec(wple, layer), _layer_spec(gple, layer), _layer_spec(wgate, layer),
                  _layer_spec(gfin, 0)],
        out_specs=pl.BlockSpec((ts, d), lambda t: (ffn_tile(t), 0)),
        scratch_shapes=[pltpu.VMEM((ts + WINDOW, 4 * LANES), BF16),
                        pltpu.VMEM((ts + WINDOW, 4 * LANES), BF16),
                        pltpu.VMEM((conv_ch // LANES, ts + CONV_HALO, LANES), F32),
                        pltpu.VMEM((ts, d), BF16),
                        pltpu.VMEM((ts, d), F32),
                        pltpu.VMEM((ts, d), BF16),
                        pltpu.VMEM((ts, d), F32)],
        compiler_params=_params(1),
        name="mix_ffn",
    )(sinks, h, q, kx, kx, vx, vx, u, u, cw, cb, lng, lnb, an, cn, wout,
      g2, wgu, wd, p, wple, gple, wgate, gfin)


def kernel(x, p, positions, ffn1_norm, ffn1_w_gate_up, ffn1_w_down, mix_norm, w_in, sinks, conv_w, conv_b, conv_ln_g, conv_ln_b, attn_out_norm, conv_out_norm, w_out, ffn2_norm, ffn2_w_gate_up, ffn2_w_down, w_ple, ple_norm, w_ple_gate, final_norm):
    b, s, d = x.shape
    depth = p.shape[0]
    n = b * s
    rows = lambda v: v.reshape(v.shape[0], 1, v.shape[1])
    bf = lambda w: w.astype(BF16)

    inv_freq = ROPE_THETA ** (-jnp.arange(0, ROT_DIM, 2, dtype=F32) / ROT_DIM)
    invf_lane = inv_freq[jnp.arange(LANES) % (ROT_DIM // 2)].reshape(1, LANES)
    group_rows = n // ROPE_GROUPS
    assert n % ROPE_GROUPS == 0 and group_rows % PROJ_ROWS == 0 and group_rows % TABLE_ROWS == 0
    pos_groups = positions.astype(F32).reshape(ROPE_GROUPS, group_rows).T
    cos, sin = _rope_tables(jnp.repeat(pos_groups, ROT_DIM, axis=1), invf_lane)

    wgu1, wd1, wgu2, wd2 = bf(ffn1_w_gate_up), bf(ffn1_w_down), bf(ffn2_w_gate_up), bf(ffn2_w_down)
    win, wout, wple, wgate = bf(w_in), bf(w_out), bf(w_ple), bf(w_ple_gate)
    g1, gm, g2, gple = rows(ffn1_norm), rows(mix_norm), rows(ffn2_norm), rows(ple_norm)
    cb, lng, lnb = rows(conv_b), rows(conv_ln_g), rows(conv_ln_b)
    an, cn = rows(attn_out_norm), rows(conv_out_norm)
    gfin = final_norm.reshape(1, 1, d)
    p_rows = p.reshape(depth, n, p.shape[3])

    h = x.reshape(n, d)
    for i in range(depth):
        h = _ffn(h, i, g1, wgu1, wd1)
        q, kx, vx, u = _in_proj(h, i, gm, win, cos, sin)
        h = _mix_ffn(h, i, q, kx, vx, u, sinks, conv_w, cb, lng, lnb, an, cn, wout,
                     g2, wgu2, wd2, p_rows, wple, gple, wgate, gfin,
                     seq_len=s, final=(i == depth - 1))
    return h.reshape(b, s, d)
```

```python
import functools

import jax
import jax.numpy as jnp
from jax import lax
from jax.experimental import pallas as pl
from jax.experimental.pallas import tpu as pltpu

F32 = jnp.float32
BF16 = jnp.bfloat16

HEAD_DIM = 64
N_HEADS = 8
N_KV_HEADS = 2
GROUP = N_HEADS // N_KV_HEADS
ATTN_W = N_HEADS * HEAD_DIM
KV_W = N_KV_HEADS * HEAD_DIM
CONV_WIDTH = 31
WINDOW = 128
ROT_DIM = HEAD_DIM // 4
ROPE_THETA = 500000.0
EPS = 1e-6
MASK_VALUE = -1e30

LANES = 128
SUBLANES = 8

FFN_ROWS = 1024
FFN_CHUNK = 256
PROJ_ROWS = 1024
MIX_ROWS = 512
CONV_SUB = 16
CONV_HALO = 32
TABLE_ROWS = 2048
VMEM_LIMIT = 56 * 1024 * 1024


def _rms(x, g):
    return x * lax.rsqrt(jnp.mean(x * x, axis=-1, keepdims=True) + EPS) * g


def _dot(a, b):
    return jnp.dot(a, b, preferred_element_type=F32)


def _dot_nt(a, b):
    return lax.dot_general(a, b, (((1,), (1,)), ((), ())), preferred_element_type=F32)


def _layer_spec(arr, layer):
    tail = arr.shape[1:]
    idx = (layer,) + (0,) * len(tail)
    return pl.BlockSpec((None,) + tail, lambda *_: idx, pipeline_mode=pl.Buffered(1))


def _params(n_axes):
    return pltpu.CompilerParams(
        dimension_semantics=("arbitrary",) * n_axes, vmem_limit_bytes=VMEM_LIMIT)


ROPE_GROUPS = LANES // ROT_DIM


def _rope_table_body(pos_ref, invf_ref, cos_ref, sin_ref):
    ang = pos_ref[...] * invf_ref[...]
    cos_ref[...] = jnp.cos(ang)
    sin_ref[...] = jnp.sin(ang)


def _rope_tables(pos, invf_lane):
    n = pos.shape[0]
    out = jax.ShapeDtypeStruct((n, LANES), F32)
    return pl.pallas_call(
        _rope_table_body,
        out_shape=(out, out),
        grid=(n // TABLE_ROWS,),
        in_specs=[pl.BlockSpec((TABLE_ROWS, LANES), lambda i: (i, 0)),
                  pl.BlockSpec((1, LANES), lambda i: (0, 0))],
        out_specs=(pl.BlockSpec((TABLE_ROWS, LANES), lambda i: (i, 0)),
                   pl.BlockSpec((TABLE_ROWS, LANES), lambda i: (i, 0))),
        compiler_params=_params(1),
        name="rope_table",
    )(pos, invf_lane)


def _swiglu_chunk(xn_ref, wgu_ref, wd_ref, c):
    ff = wd_ref.shape[0]
    lo = c * FFN_CHUNK
    xn = xn_ref[...]
    g = _dot(xn, wgu_ref[:, lo:lo + FFN_CHUNK])
    u = _dot(xn, wgu_ref[:, ff + lo:ff + lo + FFN_CHUNK])
    a = (g * jax.nn.sigmoid(g) * u).astype(BF16)
    return _dot(a, wd_ref[lo:lo + FFN_CHUNK, :])


ANCHOR_ROWS = 16


def _anchor(ref, row0, value, always):
    rows = slice(row0, row0 + ANCHOR_ROWS)
    ref[rows, :LANES] = jnp.where(always, ref[rows, :LANES], value[:ANCHOR_ROWS, :LANES].astype(ref.dtype))


def _swiglu(xn_ref, wgu_ref, wd_ref, acc_ref, units=None, always=None):
    n_chunks = wd_ref.shape[0] // FFN_CHUNK
    y = None
    d = None
    for c in range(n_chunks):
        if always is not None and d is not None:
            _anchor(xn_ref, 0, d, always)
        d = _swiglu_chunk(xn_ref, wgu_ref, wd_ref, c)
        if c == 0:
            acc_ref[...] = d
        elif c < n_chunks - 1:
            acc_ref[...] += d
        else:
            y = acc_ref[...] + d
        for thunk in (units[c] if units else ()):
            thunk()
    return y


def _ffn_body(x_ref, g_ref, wgu_ref, wd_ref, o_ref, xn_ref, acc_ref):
    xn_ref[...] = _rms(x_ref[...], g_ref[...]).astype(BF16)
    y = _swiglu(xn_ref, wgu_ref, wd_ref, acc_ref)
    o_ref[...] = x_ref[...] + 0.5 * y


def _ffn(h, layer, g, wgu, wd):
    n, d = h.shape
    row = pl.BlockSpec((FFN_ROWS, d), lambda i: (i, 0))
    return pl.pallas_call(
        _ffn_body,
        out_shape=jax.ShapeDtypeStruct((n, d), F32),
        grid=(n // FFN_ROWS,),
        in_specs=[row, _layer_spec(g, layer), _layer_spec(wgu, layer), _layer_spec(wd, layer)],
        out_specs=row,
        scratch_shapes=[pltpu.VMEM((FFN_ROWS, d), BF16), pltpu.VMEM((FFN_ROWS, d), F32)],
        compiler_params=_params(1),
        name="ffn",
    )(h, g, wgu, wd)


def _in_proj_body(h_ref, g_ref, win_ref, cos_ref, sin_ref, q_ref, kx_ref, vx_ref, u_ref, *, tiles_per_group):
    xn = _rms(h_ref[...], g_ref[...]).astype(BF16)
    lane = lax.broadcasted_iota(jnp.int32, (1, LANES), 1)
    in_head = lane % HEAD_DIM
    half = ROT_DIM // 2
    group = pl.program_id(0) // tiles_per_group
    src = jnp.broadcast_to(group * ROT_DIM + jnp.where(in_head < ROT_DIM, in_head, 0), cos_ref.shape)
    cos = jnp.where(in_head < ROT_DIM, jnp.take_along_axis(cos_ref[...], src, axis=1), 1.0)
    sin = jnp.take_along_axis(sin_ref[...], src, axis=1)
    s_lo = jnp.where(in_head < half, -sin, 0.0)
    s_hi = jnp.where((in_head >= half) & (in_head < ROT_DIM), sin, 0.0)

    def rope(t):
        return (t * cos + pltpu.roll(t, LANES - half, 1) * s_lo + pltpu.roll(t, half, 1) * s_hi)

    k_col = ATTN_W
    v_col = k_col + KV_W
    a_col = v_col + KV_W
    conv_ch = (win_ref.shape[1] - a_col) // 2

    scale = HEAD_DIM ** -0.5
    q = _dot(xn, win_ref[:, :k_col])
    for p in range(ATTN_W // LANES):
        cols = slice(p * LANES, (p + 1) * LANES)
        q_ref[:, cols] = (rope(q[:, cols]) * scale).astype(BF16)

    kv = _dot(xn, win_ref[:, k_col:a_col])
    low = lane < HEAD_DIM

    def head_pair_layout(t, out_ref):
        sw = pltpu.roll(t, HEAD_DIM, 1)
        out_ref[:, 0 * LANES:1 * LANES] = jnp.where(low, t, 0.0).astype(BF16)
        out_ref[:, 1 * LANES:2 * LANES] = jnp.where(low, 0.0, sw).astype(BF16)
        out_ref[:, 2 * LANES:3 * LANES] = jnp.where(low, sw, 0.0).astype(BF16)
        out_ref[:, 3 * LANES:4 * LANES] = jnp.where(low, 0.0, t).astype(BF16)

    head_pair_layout(rope(kv[:, :KV_W]), kx_ref)
    head_pair_layout(kv[:, KV_W:], vx_ref)

    ag = _dot(xn, win_ref[:, a_col:])
    u_ref[...] = ag[:, :conv_ch] * jax.nn.sigmoid(ag[:, conv_ch:])


def _in_proj(h, layer, g, win, cos, sin):
    n, d = h.shape
    rows = lambda w: pl.BlockSpec((PROJ_ROWS, w), lambda i: (i, 0))
    conv_ch = (win.shape[2] - ATTN_W - 2 * KV_W) // 2
    tiles_per_group = cos.shape[0] // PROJ_ROWS
    table = pl.BlockSpec((PROJ_ROWS, LANES), lambda i: (i % tiles_per_group, 0))
    return pl.pallas_call(
        functools.partial(_in_proj_body, tiles_per_group=tiles_per_group),
        out_shape=(jax.ShapeDtypeStruct((n, ATTN_W), BF16),
                   jax.ShapeDtypeStruct((n, 4 * LANES), BF16),
                   jax.ShapeDtypeStruct((n, 4 * LANES), BF16),
                   jax.ShapeDtypeStruct((n, conv_ch), F32)),
        grid=(n // PROJ_ROWS,),
        in_specs=[rows(d), _layer_spec(g, layer), _layer_spec(win, layer), table, table],
        out_specs=(rows(ATTN_W), rows(4 * LANES), rows(4 * LANES), rows(conv_ch)),
        compiler_params=_params(1),
        name="in_proj",
    )(h, g, win, cos, sin)


def _conv_rows(uall_ref, cw_ref, base):
    first_tap = CONV_HALO - (CONV_WIDTH - 1)
    tiles = []
    for j in range(uall_ref.shape[0]):
        lanes = slice(j * LANES, (j + 1) * LANES)
        acc = None
        for k in range(CONV_WIDTH):
            start = base + first_tap + k
            term = cw_ref[k:k + 1, lanes] * uall_ref[j, start:start + CONV_SUB, :]
            acc = term if acc is None else acc + term
        tiles.append(acc)
    return jnp.concatenate(tiles, axis=1)


def _mix_ffn_body(sinks_ref, h_ref, q_ref, kc_ref, kp_ref, vc_ref, vp_ref, uc_ref, up_ref,
                  cw_ref, cb_ref, lng_ref, lnb_ref, an_ref, cn_ref, wout_ref,
                  g2_ref, wgu_ref, wd_ref, p_ref, wple_ref, gple_ref, wgate_ref, gfin_ref,
                  o_ref, kall_ref, vall_ref, uall_ref, mixed_ref, hmid_ref, xn_ref, acc_ref,
                  *, layer, tiles_per_seq, n_tiles, final):
    t = pl.program_id(0)
    ts = h_ref.shape[0]
    not_first = jnp.minimum(t, n_tiles - 1) % tiles_per_seq > 0

    @pl.when(t == 0)
    def _():
        hmid_ref[...] = jnp.zeros_like(hmid_ref)

    h_prev = hmid_ref[...]
    o_ref[...] = h_prev
    xn_ref[...] = _rms(h_prev, g2_ref[...]).astype(BF16)
    e = _rms(_dot(p_ref[...].astype(BF16), wple_ref[...]), gple_ref[...])

    kall_ref[:WINDOW] = kp_ref[...]
    kall_ref[WINDOW:] = kc_ref[...]
    vall_ref[:WINDOW] = vp_ref[...]
    vall_ref[WINDOW:] = vc_ref[...]
    for j in range(uall_ref.shape[0]):
        lanes = slice(j * LANES, (j + 1) * LANES)
        uall_ref[j, :CONV_HALO, :] = jnp.where(not_first, up_ref[:, lanes], 0.0)
        uall_ref[j, CONV_HALO:, :] = uc_ref[:, lanes]

    r = lax.broadcasted_iota(jnp.int32, (WINDOW, WINDOW), 0)
    c = lax.broadcasted_iota(jnp.int32, (WINDOW, WINDOW), 1)
    from_prev = c > r
    no_prev = from_prev & jnp.logical_not(not_first)

    def kv_tiles(p):
        grp = (2 * p) // GROUP
        return (slice((2 * grp) * LANES, (2 * grp + 1) * LANES),
                slice((2 * grp + 1) * LANES, (2 * grp + 2) * LANES))

    def softmax_sink(s, first_block, sink):
        s = jnp.where(from_prev, s[:, :WINDOW], s[:, WINDOW:])
        if first_block:
            s = jnp.where(no_prev, MASK_VALUE, s)
        m = jnp.maximum(jnp.max(s, axis=-1, keepdims=True), sink)
        e = jnp.exp(s - m)
        den = jnp.sum(e, axis=-1, keepdims=True) + jnp.exp(sink - m)
        p = e * (1.0 / den)
        return jnp.concatenate([jnp.where(from_prev, p, 0.0), jnp.where(from_prev, 0.0, p)],
                               axis=1).astype(BF16)

    always = t >= 0
    chain = {}

    def chain_after(ref, row0):
        if "token" in chain:
            _anchor(ref, row0, chain["token"], always)

    def conv_module(sb):
        base = sb * CONV_SUB
        chain_after(uall_ref.at[0], base)
        y = _conv_rows(uall_ref, cw_ref, base) + cb_ref[...]
        chain["token"] = y
        mu = jnp.mean(y, axis=-1, keepdims=True)
        yc = y - mu
        yn = yc * lax.rsqrt(jnp.mean(yc * yc, axis=-1, keepdims=True) + EPS) * lng_ref[...] + lnb_ref[...]
        z = yn * jax.nn.sigmoid(yn)
        out = _rms(z, cn_ref[...]).astype(BF16)
        mixed_ref[base:base + CONV_SUB, ATTN_W:] = out

    conv_per_block = WINDOW // CONV_SUB
    pairs_per_group = GROUP // 2
    stages = []
    for qb in range(ts // WINDOW):
        rows = slice(qb * WINDOW, (qb + 1) * WINDOW)
        keys = slice(qb * WINDOW, (qb + 2) * WINDOW)
        first_block = qb == 0
        state = {}

        def scores_stage(rows=rows, keys=keys, state=state):
            chain_after(kall_ref, keys.start)
            scores = []
            for grp in range(N_KV_HEADS):
                pairs = range(grp * pairs_per_group, (grp + 1) * pairs_per_group)
                lo, hi = kv_tiles(pairs[0])
                qg = jnp.concatenate([q_ref[rows, p * LANES:(p + 1) * LANES] for p in pairs], axis=0)
                s_even = _dot_nt(qg, kall_ref[keys, lo])
                s_odd = _dot_nt(qg, kall_ref[keys, hi])
                for k in range(pairs_per_group):
                    scores += [s_even[k * WINDOW:(k + 1) * WINDOW], s_odd[k * WINDOW:(k + 1) * WINDOW]]
            state["scores"] = scores
            chain["token"] = scores[0]

        def softmax_stage(first_block=first_block, state=state):
            state["probs"] = [softmax_sink(s, first_block, sinks_ref[layer, hd])
                              for hd, s in enumerate(state.pop("scores"))]
            chain["token"] = state["probs"][0]

        def values_stage(rows=rows, keys=keys, state=state):
            chain_after(vall_ref, keys.start)
            probs = state.pop("probs")
            outs = []
            for grp in range(N_KV_HEADS):
                pairs = range(grp * pairs_per_group, (grp + 1) * pairs_per_group)
                lo, hi = kv_tiles(pairs[0])
                p_even = jnp.concatenate([probs[2 * p] for p in pairs], axis=0)
                p_odd = jnp.concatenate([probs[2 * p + 1] for p in pairs], axis=0)
                og = _dot(p_even, vall_ref[keys, lo]) + _dot(p_odd, vall_ref[keys, hi])
                outs += [og[k * WINDOW:(k + 1) * WINDOW] for k in range(pairs_per_group)]
            mixed_ref[rows, :ATTN_W] = _rms(jnp.concatenate(outs, axis=1), an_ref[...]).astype(BF16)
            chain["token"] = outs[0]

        attn_stages = [scores_stage, softmax_stage, values_stage]
        conv_stages = [functools.partial(conv_module, qb * conv_per_block + sb)
                       for sb in range(conv_per_block)]
        done = 0
        for k, stage in enumerate(attn_stages):
            stages.append(stage)
            due = (k + 1) * len(conv_stages) // len(attn_stages)
            stages += conv_stages[done:due]
            done = due

    n_chunks = wd_ref.shape[0] // FFN_CHUNK
    per_chunk = -(-len(stages) // n_chunks)
    units = [stages[c * per_chunk:(c + 1) * per_chunk] for c in range(n_chunks)]
    y = _swiglu(xn_ref, wgu_ref, wd_ref, acc_ref, units, always)

    hmid_ref[...] = h_ref[...] + _dot(mixed_ref[...], wout_ref[...])

    h = o_ref[...] + 0.5 * y
    h = h + jax.nn.sigmoid(_dot(h.astype(BF16), wgate_ref[...])) * e
    if final:
        h = _rms(h, gfin_ref[...])
    o_ref[...] = h


def _mix_ffn(h, layer, q, kx, vx, u, sinks, cw, cb, lng, lnb, an, cn, wout,
             g2, wgu, wd, p, wple, gple, wgate, gfin, *, seq_len, final):
    n, d = h.shape
    conv_ch = u.shape[1]
    ts = MIX_ROWS
    n_tiles = n // ts
    mix_tile = lambda t: jnp.minimum(t, n_tiles - 1)
    ffn_tile = lambda t: jnp.maximum(t - 1, 0)
    cur = lambda w: pl.BlockSpec((ts, w), lambda t: (mix_tile(t), 0))
    prev = lambda rows, w: pl.BlockSpec(
        (rows, w), lambda t: (jnp.maximum(mix_tile(t) * (ts // rows) - 1, 0), 0))
    return pl.pallas_call(
        functools.partial(_mix_ffn_body, layer=layer, tiles_per_seq=seq_len // ts, n_tiles=n_tiles,
                          final=final),
        out_shape=jax.ShapeDtypeStruct((n, d), F32),
        grid=(n_tiles + 1,),
        in_specs=[pl.BlockSpec(memory_space=pltpu.SMEM),
                  cur(d), cur(ATTN_W),
                  cur(4 * LANES), prev(WINDOW, 4 * LANES),
                  cur(4 * LANES), prev(WINDOW, 4 * LANES),
                  cur(conv_ch), prev(CONV_HALO, conv_ch),
                  _layer_spec(cw, layer), _layer_spec(cb, layer), _layer_spec(lng, layer),
                  _layer_spec(lnb, layer), _layer_spec(an, layer), _layer_spec(cn, layer),
                  _layer_spec(wout, layer),
                  _layer_spec(g2, layer), _layer_spec(wgu, layer), _layer_spec(wd, layer),
                  pl.BlockSpec((None, ts, p.shape[2]), lambda t: (layer, ffn_tile(t), 0)),
                  _layer_spec(wple, layer), _layer_spec(gple, layer), _layer_spec(wgate, layer),
                  _layer_spec(gfin, 0)],
        out_specs=pl.BlockSpec((ts, d), lambda t: (ffn_tile(t), 0)),
        scratch_shapes=[pltpu.VMEM((ts + WINDOW, 4 * LANES), BF16),
                        pltpu.VMEM((ts + WINDOW, 4 * LANES), BF16),
                        pltpu.VMEM((conv_ch // LANES, ts + CONV_HALO, LANES), F32),
                        pltpu.VMEM((ts, d), BF16),
                        pltpu.VMEM((ts, d), F32),
                        pltpu.VMEM((ts, d), BF16),
                        pltpu.VMEM((ts, d), F32)],
        compiler_params=_params(1),
        name="mix_ffn",
    )(sinks, h, q, kx, kx, vx, vx, u, u, cw, cb, lng, lnb, an, cn, wout,
      g2, wgu, wd, p, wple, gple, wgate, gfin)


def kernel(x, p, positions, ffn1_norm, ffn1_w_gate_up, ffn1_w_down, mix_norm, w_in, sinks, conv_w, conv_b, conv_ln_g, conv_ln_b, attn_out_norm, conv_out_norm, w_out, ffn2_norm, ffn2_w_gate_up, ffn2_w_down, w_ple, ple_norm, w_ple_gate, final_norm):
    b, s, d = x.shape
    depth = p.shape[0]
    n = b * s
    rows = lambda v: v.reshape(v.shape[0], 1, v.shape[1])
    bf = lambda w: w.astype(BF16)

    inv_freq = ROPE_THETA ** (-jnp.arange(0, ROT_DIM, 2, dtype=F32) / ROT_DIM)
    invf_lane = inv_freq[jnp.arange(LANES) % (ROT_DIM // 2)].reshape(1, LANES)
    group_rows = n // ROPE_GROUPS
    assert n % ROPE_GROUPS == 0 and group_rows % PROJ_ROWS == 0 and group_rows % TABLE_ROWS == 0
    pos_groups = positions.astype(F32).reshape(ROPE_GROUPS, group_rows).T
    cos, sin = _rope_tables(jnp.repeat(pos_groups, ROT_DIM, axis=1), invf_lane)

    wgu1, wd1, wgu2, wd2 = bf(ffn1_w_gate_up), bf(ffn1_w_down), bf(ffn2_w_gate_up), bf(ffn2_w_down)
    win, wout, wple, wgate = bf(w_in), bf(w_out), bf(w_ple), bf(w_ple_gate)
    g1, gm, g2, gple = rows(ffn1_norm), rows(mix_norm), rows(ffn2_norm), rows(ple_norm)
    cb, lng, lnb = rows(conv_b), rows(conv_ln_g), rows(conv_ln_b)
    an, cn = rows(attn_out_norm), rows(conv_out_norm)
    gfin = final_norm.reshape(1, 1, d)
    p_rows = p.reshape(depth, n, p.shape[3])

    h = x.reshape(n, d)
    for i in range(depth):
        h = _ffn(h, i, g1, wgu1, wd1)
        q, kx, vx, u = _in_proj(h, i, gm, win, cos, sin)
        h = _mix_ffn(h, i, q, kx, vx, u, sinks, conv_w, cb, lng, lnb, an, cn, wout,
                     g2, wgu2, wd2, p_rows, wple, gple, wgate, gfin,
                     seq_len=s, final=(i == depth - 1))
    return h.reshape(b, s, d)
```
